```python
import math
import jax
import jax.numpy as jnp
from jax import lax
import numpy as np

D_MODEL = 1024
BATCH = 32
SEQ = 2048
DEPTH = 2
DEC_BATCH = 32
DEC_SEQ = 64
PAST_LEN = 4096

CHUNK = 64
N_A = DEPTH // 2
N_B = DEPTH - N_A
GROUP = 16
N_GROUPS = D_MODEL // GROUP
P_STATE = 64
HEAD_DIM = 64
V_DIM = 2 * HEAD_DIM
N_HEADS = D_MODEL // V_DIM
D_FF = ((8 * D_MODEL // 3 + 127) // 128) * 128
ROPE_THETA = 10000.0
Q_BLOCK = 128
ALPHA = (2 * DEPTH) ** 0.25
BETA = (8 * DEPTH) ** -0.25
LN_EPS = 1e-5
RMS_EPS = 1e-5

kernel_name = 'yoco_s5_diffattn_streaming_step'


def layer_norm(x, g, b):
    xf = x.astype(jnp.float32)
    mu = jnp.mean(xf, axis=-1, keepdims=True)
    var = jnp.mean(jnp.square(xf - mu), axis=-1, keepdims=True)
    return ((xf - mu) * lax.rsqrt(var + LN_EPS) * g + b).astype(x.dtype)


def swiglu(x, w_in, w_out):
    a, gate = jnp.split(x @ w_in, 2, axis=-1)
    return (jax.nn.silu(a) * gate) @ w_out


def rope(x, pos):
    half = HEAD_DIM // 2
    inv = ROPE_THETA ** (-jnp.arange(half, dtype=jnp.float32) / half)
    ang = pos.astype(jnp.float32)[:, None] * inv[None, :]
    cos = jnp.cos(ang)[:, None, :]
    sin = jnp.sin(ang)[:, None, :]
    xf = x.astype(jnp.float32)
    x1, x2 = xf[..., :half], xf[..., half:]
    return jnp.concatenate([x1 * cos - x2 * sin, x1 * sin + x2 * cos], axis=-1).astype(x.dtype)


def s5_discretise(a_re, a_im, log_dt, b_re, b_im):
    f32 = jnp.float32
    lam_re = jnp.minimum(a_re.astype(f32), -1e-4)
    lam_im = a_im.astype(f32)
    dt = jnp.exp(log_dt.astype(f32))[:, None]
    mag = jnp.exp(lam_re * dt)
    lb_re = mag * jnp.cos(lam_im * dt)
    lb_im = mag * jnp.sin(lam_im * dt)
    den = lam_re * lam_re + lam_im * lam_im
    n_re = lb_re - 1.0
    c_re = (n_re * lam_re + lb_im * lam_im) / den
    c_im = (lb_im * lam_re - n_re * lam_im) / den
    b_re = b_re.astype(f32)
    b_im = b_im.astype(f32)
    bb_re = c_re[..., None] * b_re - c_im[..., None] * b_im
    bb_im = c_re[..., None] * b_im + c_im[..., None] * b_re
    return lb_re, lb_im, bb_re, bb_im


def complex_affine_combine(e1, e2):
    a1r, a1i, b1r, b1i = e1
    a2r, a2i, b2r, b2i = e2
    return (a2r * a1r - a2i * a1i,
            a2r * a1i + a2i * a1r,
            a2r * b1r - a2i * b1i + b2r,
            a2r * b1i + a2i * b1r + b2i)


def s5_mixer(x, h0_re, h0_im, a_re, a_im, log_dt, b_re, b_im, c_re, c_im, d_skip, w_glu):
    B, L, _ = x.shape
    f32 = jnp.float32
    lb_re, lb_im, bb_re, bb_im = s5_discretise(a_re, a_im, log_dt, b_re, b_im)
    c_re = c_re.astype(f32)
    c_im = c_im.astype(f32)
    blk = min(CHUNK, L)
    n_blk = L // blk
    u = x.astype(f32)
    u_blocks = u.reshape(B, n_blk, blk, N_GROUPS, GROUP).transpose(1, 0, 2, 3, 4)

    def block_step(carry, u_blk):
        h_re, h_im = carry
        bu_re = jnp.einsum('gpc,btgc->btgp', bb_re, u_blk)
        bu_im = jnp.einsum('gpc,btgc->btgp', bb_im, u_blk)
        bu_re = bu_re.at[:, 0].add(lb_re * h_re - lb_im * h_im)
        bu_im = bu_im.at[:, 0].add(lb_re * h_im + lb_im * h_re)
        a_r = jnp.broadcast_to(lb_re, bu_re.shape)
        a_i = jnp.broadcast_to(lb_im, bu_im.shape)
        _, _, hs_re, hs_im = lax.associative_scan(complex_affine_combine, (a_r, a_i, bu_re, bu_im), axis=1)
        y_blk = jnp.einsum('gcp,btgp->btgc', c_re, hs_re) - jnp.einsum('gcp,btgp->btgc', c_im, hs_im)
        return (hs_re[:, -1], hs_im[:, -1]), y_blk

    (h_re, h_im), y = lax.scan(block_step, (h0_re.astype(f32), h0_im.astype(f32)), u_blocks)
    y = y.transpose(1, 0, 2, 3, 4).reshape(B, L, D_MODEL) + d_skip.astype(f32) * u
    z = jax.nn.gelu(y).astype(x.dtype)
    out, gate = jnp.split(z @ w_glu, 2, axis=-1)
    return out * jax.nn.sigmoid(gate), h_re, h_im


def shared_kv(h, pos, w_kv):
    B, L, _ = h.shape
    k, v = jnp.split(h @ w_kv, 2, axis=-1)
    k = rope(k.reshape(B, L, 2 * N_HEADS, HEAD_DIM), pos).reshape(B, L, N_HEADS, 2, HEAD_DIM)
    return k, v.reshape(B, L, N_HEADS, V_DIM)


def diff_attend(q, q_pos, segments, lam):
    scale = HEAD_DIM ** -0.5
    scores = []
    for k, _, k_pos in segments:
        s = jnp.einsum('bqhmd,bkhmd->bhmqk', q, k).astype(jnp.float32) * scale
        visible = (k_pos[None, :] // CHUNK) <= (q_pos[:, None] // CHUNK)
        scores.append(jnp.where(visible, s, -1e30))
    p = jax.nn.softmax(jnp.concatenate(scores, axis=-1), axis=-1)
    p = p[:, :, 0] - lam * p[:, :, 1]
    out = 0.0
    start = 0
    for k, v, _ in segments:
        n = k.shape[1]
        out = out + jnp.einsum('bhqk,bkhe->bqhe', p[..., start:start + n].astype(v.dtype), v)
        start += n
    return out


def diff_attention(x, pos, k_sh, v_sh, k_past, v_past, pos_past, w_q, lam_vec, subln_g, w_o, layer_idx):
    B, L, _ = x.shape
    q = rope((x @ w_q).reshape(B, L, 2 * N_HEADS, HEAD_DIM), pos).reshape(B, L, N_HEADS, 2, HEAD_DIM)
    lam_init = 0.8 - 0.6 * math.exp(-0.3 * layer_idx)
    lv = lam_vec.astype(jnp.float32)
    lam = jnp.exp(jnp.sum(lv[0] * lv[1])) - jnp.exp(jnp.sum(lv[2] * lv[3])) + lam_init
    segments = [(k_sh, v_sh, pos)]
    if k_past is not None:
        segments = [(k_past, v_past, pos_past)] + segments
    if L <= Q_BLOCK:
        o = diff_attend(q, pos, segments, lam)
    else:
        n_qb = L // Q_BLOCK
        q_blocks = q.reshape(B, n_qb, Q_BLOCK, N_HEADS, 2, HEAD_DIM).transpose(1, 0, 2, 3, 4, 5)
        pos_blocks = pos.reshape(n_qb, Q_BLOCK)
        o = lax.map(lambda qp: diff_attend(qp[0], qp[1], segments, lam), (q_blocks, pos_blocks))
        o = o.transpose(1, 0, 2, 3, 4).reshape(B, L, N_HEADS, V_DIM)
    of = o.astype(jnp.float32)
    of = of * lax.rsqrt(jnp.mean(of * of, axis=-1, keepdims=True) + RMS_EPS) * subln_g * (1.0 - lam_init)
    return of.astype(x.dtype).reshape(B, L, D_MODEL) @ w_o


def run_trunk(x, pos, h0_re, h0_im, k_past, v_past, pos_past, w):
    new_re, new_im = [], []
    k_sh = v_sh = None
    for l in range(DEPTH):
        if l == N_A:
            k_sh, v_sh = shared_kv(x, pos, w['attn_w_kv'])
        g, b = w['ln_g'][l], w['ln_b'][l]
        x = layer_norm(ALPHA * x + 0.5 * swiglu(x, w['ffn_w_in'][l, 0], w['ffn_w_out'][l, 0]), g[0], b[0])
        if l < N_A:
            y, hr, hi = s5_mixer(x, h0_re[l], h0_im[l], w['ssm_a_re'][l], w['ssm_a_im'][l], w['ssm_log_dt'][l],
                                 w['ssm_b_re'][l], w['ssm_b_im'][l], w['ssm_c_re'][l], w['ssm_c_im'][l],
                                 w['ssm_d'][l], w['ssm_w_glu'][l])
            new_re.append(hr)
            new_im.append(hi)
        else:
            j = l - N_A
            y = diff_attention(x, pos, k_sh, v_sh, k_past, v_past, pos_past, w['attn_w_q'][j], w['attn_lam'][j],
                               w['attn_subln_g'][j], w['attn_w_o'][j], l)
        x = layer_norm(ALPHA * x + y, g[1], b[1])
        x = layer_norm(ALPHA * x + 0.5 * swiglu(x, w['ffn_w_in'][l, 1], w['ffn_w_out'][l, 1]), g[2], b[2])
    return x, jnp.stack(new_re), jnp.stack(new_im), k_sh, v_sh


def setup_inputs(seed: int = 0) -> dict:
    key = jax.random.key(seed)
    ks = jax.random.split(key, 24)
    f32 = jnp.float32

    def nrm(k, shape, s):
        return jax.random.normal(k, shape, f32) * s

    D = D_MODEL
    x_prompt = nrm(ks[0], (BATCH, SEQ, D), 1.0)
    x_sample = nrm(ks[1], (DEC_BATCH, DEC_SEQ, D), 1.0)
    cache_k = nrm(ks[2], (DEC_BATCH, PAST_LEN, N_HEADS, 2, HEAD_DIM), 1.0)
    cache_v = nrm(ks[3], (DEC_BATCH, PAST_LEN, N_HEADS, V_DIM), 1.0)
    state_ssm_re = nrm(ks[4], (N_A, DEC_BATCH, N_GROUPS, P_STATE), 0.1)
    state_ssm_im = nrm(ks[5], (N_A, DEC_BATCH, N_GROUPS, P_STATE), 0.1)
    ln_g = 1.0 + nrm(ks[6], (DEPTH, 3, D), 0.01)
    ln_b = nrm(ks[7], (DEPTH, 3, D), 0.01)
    ffn_w_in = nrm(ks[8], (DEPTH, 2, D, 2 * D_FF), D ** -0.5)
    ffn_w_out = nrm(ks[9], (DEPTH, 2, D_FF, D), (D_FF ** -0.5) * BETA)
    ssm_a_re = -0.5 + nrm(ks[10], (N_A, N_GROUPS, P_STATE), 0.01)
    ssm_a_im = math.pi * jnp.arange(P_STATE, dtype=f32) + nrm(ks[11], (N_A, N_GROUPS, P_STATE), 0.01)
    ssm_log_dt = jax.random.uniform(ks[12], (N_A, N_GROUPS), f32, math.log(1e-3), math.log(1e-1))
    ssm_b_re = nrm(ks[13], (N_A, N_GROUPS, P_STATE, GROUP), (2 * GROUP) ** -0.5)
    ssm_b_im = nrm(ks[14], (N_A, N_GROUPS, P_STATE, GROUP), (2 * GROUP) ** -0.5)
    ssm_c_re = nrm(ks[15], (N_A, N_GROUPS, GROUP, P_STATE), P_STATE ** -0.5)
    ssm_c_im = nrm(ks[16], (N_A, N_GROUPS, GROUP, P_STATE), P_STATE ** -0.5)
    ssm_d = nrm(ks[17], (N_A, D), 1.0)
    glu_scale = jnp.concatenate([jnp.full((D,), BETA, f32), jnp.ones((D,), f32)])
    ssm_w_glu = nrm(ks[18], (N_A, D, 2 * D), D ** -0.5) * glu_scale
    attn_w_q = nrm(ks[19], (N_B, D, D), D ** -0.5)
    kv_scale = jnp.concatenate([jnp.ones((D,), f32), jnp.full((D,), BETA, f32)])
    attn_w_kv = nrm(ks[20], (D, 2 * D), D ** -0.5) * kv_scale
    attn_lam = nrm(ks[21], (N_B, 4, HEAD_DIM), 0.1)
    attn_subln_g = 1.0 + nrm(ks[22], (N_B, V_DIM), 0.01)
    attn_w_o = nrm(ks[23], (N_B, D, D), (D ** -0.5) * BETA)
    return {'x_prompt': x_prompt, 'x_sample': x_sample, 'cache_k': cache_k, 'cache_v': cache_v,
            'state_ssm_re': state_ssm_re, 'state_ssm_im': state_ssm_im,
            'ln_g': ln_g, 'ln_b': ln_b, 'ffn_w_in': ffn_w_in, 'ffn_w_out': ffn_w_out,
            'ssm_a_re': ssm_a_re, 'ssm_a_im': ssm_a_im, 'ssm_log_dt': ssm_log_dt,
            'ssm_b_re': ssm_b_re, 'ssm_b_im': ssm_b_im, 'ssm_c_re': ssm_c_re, 'ssm_c_im': ssm_c_im,
            'ssm_d': ssm_d, 'ssm_w_glu': ssm_w_glu,
            'attn_w_q': attn_w_q, 'attn_w_kv': attn_w_kv, 'attn_lam': attn_lam,
            'attn_subln_g': attn_subln_g, 'attn_w_o': attn_w_o}


def reference(x_prompt, x_sample, cache_k, cache_v, state_ssm_re, state_ssm_im,
              ln_g, ln_b, ffn_w_in, ffn_w_out,
              ssm_a_re, ssm_a_im, ssm_log_dt, ssm_b_re, ssm_b_im, ssm_c_re, ssm_c_im, ssm_d, ssm_w_glu,
              attn_w_q, attn_w_kv, attn_lam, attn_subln_g, attn_w_o):
    w = {'ln_g': ln_g, 'ln_b': ln_b, 'ffn_w_in': ffn_w_in, 'ffn_w_out': ffn_w_out,
         'ssm_a_re': ssm_a_re, 'ssm_a_im': ssm_a_im, 'ssm_log_dt': ssm_log_dt,
         'ssm_b_re': ssm_b_re, 'ssm_b_im': ssm_b_im, 'ssm_c_re': ssm_c_re, 'ssm_c_im': ssm_c_im,
         'ssm_d': ssm_d, 'ssm_w_glu': ssm_w_glu,
         'attn_w_q': attn_w_q, 'attn_w_kv': attn_w_kv, 'attn_lam': attn_lam,
         'attn_subln_g': attn_subln_g, 'attn_w_o': attn_w_o}
    b_p, l_p, _ = x_prompt.shape
    pos_p = jnp.arange(l_p, dtype=jnp.int32)
    h0 = jnp.zeros((N_A, b_p, N_GROUPS, P_STATE), jnp.float32)
    y_prompt, p_ssm_re, p_ssm_im, p_k, p_v = run_trunk(x_prompt, pos_p, h0, h0, None, None, None, w)
    past = cache_k.shape[1]
    pos_past = jnp.arange(past, dtype=jnp.int32)
    pos_s = past + jnp.arange(x_sample.shape[1], dtype=jnp.int32)
    y_sample, s_ssm_re, s_ssm_im, s_k, s_v = run_trunk(x_sample, pos_s, state_ssm_re, state_ssm_im,
                                                       cache_k, cache_v, pos_past, w)
    return (y_prompt, y_sample, p_ssm_re, p_ssm_im, p_k, p_v, s_ssm_re, s_ssm_im, s_k, s_v)
```

```python
import functools
import math

import jax
import jax.numpy as jnp
from jax import lax
from jax.experimental import pallas as pl
from jax.experimental.pallas import tpu as pltpu

F32 = jnp.float32
BF16 = jnp.bfloat16

D_MODEL = 1024
DEPTH = 2
CHUNK = 64
N_A = DEPTH // 2
GROUP = 16
N_GROUPS = D_MODEL // GROUP
P_STATE = 64
HEAD_DIM = 64
V_DIM = 2 * HEAD_DIM
N_HEADS = D_MODEL // V_DIM
D_FF = ((8 * D_MODEL // 3 + 127) // 128) * 128
ROPE_THETA = 10000.0
ALPHA = (2 * DEPTH) ** 0.25
LN_EPS = 1e-5
RMS_EPS = 1e-5
NEG_INF = -1e30

LANES = 128
ROW_TILE = 512
FF_TILE = 256
S5_T = 16
Q_TILE = 256
VMEM_LIMIT = 56 * 1024 * 1024


def _params(n_axes):
    return pltpu.CompilerParams(dimension_semantics=("arbitrary",) * n_axes, vmem_limit_bytes=VMEM_LIMIT)


def _row_tile(n):
    return ROW_TILE if n % ROW_TILE == 0 else n


def _const_spec(shape):
    return pl.BlockSpec(shape, lambda *_: (0,) * len(shape), pipeline_mode=pl.Buffered(1))


def _layer_norm(v, g, b):
    mu = jnp.mean(v, axis=-1, keepdims=True)
    c = v - mu
    var = jnp.mean(c * c, axis=-1, keepdims=True)
    return c * lax.rsqrt(var + LN_EPS) * g + b


def _sigmoid(v):
    return 0.5 * jnp.tanh(0.5 * v) + 0.5


def _gelu_tanh(v):
    return 0.5 * v * (1.0 + jnp.tanh(math.sqrt(2.0 / math.pi) * (v + 0.044715 * (v * v * v))))


def _dot(a, b):
    return jnp.dot(a, b, preferred_element_type=F32)


def _dot_nt(a, b):
    return lax.dot_general(a, b, (((1,), (1,)), ((), ())), preferred_element_type=F32)


def _ffn_ln_body(x_ref, win_ref, wout_ref, g_ref, b_ref, o_ref):
    x = x_ref[...]
    xb = x.astype(BF16)
    acc = None
    for c in range(D_FF // FF_TILE):
        a = _dot(xb, win_ref[:, pl.ds(c * FF_TILE, FF_TILE)])
        gate = _dot(xb, win_ref[:, pl.ds(D_FF + c * FF_TILE, FF_TILE)])
        h = (a * _sigmoid(a) * gate).astype(BF16)
        part = _dot(h, wout_ref[pl.ds(c * FF_TILE, FF_TILE), :])
        acc = part if acc is None else acc + part
    o_ref[...] = _layer_norm(ALPHA * x + 0.5 * acc, g_ref[...], b_ref[...])


def _ffn_ln(x, w_in, w_out, g, b):
    n = x.shape[0]
    tm = _row_tile(n)
    row = pl.BlockSpec((tm, D_MODEL), lambda i: (i, 0))
    return pl.pallas_call(
        _ffn_ln_body,
        grid=(n // tm,),
        in_specs=[row, _const_spec((D_MODEL, 2 * D_FF)), _const_spec((D_FF, D_MODEL)),
                  _const_spec((1, D_MODEL)), _const_spec((1, D_MODEL))],
        out_specs=row,
        out_shape=jax.ShapeDtypeStruct((n, D_MODEL), F32),
        compiler_params=_params(1),
        name="ffn_ln",
    )(x, w_in, w_out, g, b)


def _s5_weights(a_re, a_im, log_dt, b_re, b_im, c_re, c_im):
    hi = lax.Precision.HIGHEST
    lam_re = jnp.minimum(a_re.astype(F32), -1e-4)
    lam_im = a_im.astype(F32)
    dt = jnp.exp(log_dt.astype(F32))[:, None]
    mag = jnp.exp(lam_re * dt)
    lb_re = mag * jnp.cos(lam_im * dt)
    lb_im = mag * jnp.sin(lam_im * dt)
    den = lam_re * lam_re + lam_im * lam_im
    n_re = lb_re - 1.0
    k_re = (n_re * lam_re + lb_im * lam_im) / den
    k_im = (lb_im * lam_re - n_re * lam_im) / den
    b_re = b_re.astype(F32)
    b_im = b_im.astype(F32)
    bb_re = k_re[..., None] * b_re - k_im[..., None] * b_im
    bb_im = k_re[..., None] * b_im + k_im[..., None] * b_re
    c_re = c_re.astype(F32)
    c_im = c_im.astype(F32)

    def power_step(carry, _):
        pr, pi = carry
        return (pr * lb_re - pi * lb_im, pr * lb_im + pi * lb_re), (pr, pi)

    _, (pw_re, pw_im) = lax.scan(power_step, (jnp.ones_like(lb_re), jnp.zeros_like(lb_re)), None,
                                 length=S5_T + 1)

    cl_re = c_re[None] * pw_re[:, :, None, :] - c_im[None] * pw_im[:, :, None, :]
    cl_im = c_re[None] * pw_im[:, :, None, :] + c_im[None] * pw_re[:, :, None, :]

    taps = (jnp.einsum('ngdp,gpc->ngdc', cl_re[:S5_T], bb_re, precision=hi)
            - jnp.einsum('ngdp,gpc->ngdc', cl_im[:S5_T], bb_im, precision=hi))
    taps = jnp.concatenate([taps, jnp.zeros_like(taps[:1])], axis=0)
    s_idx = jnp.arange(S5_T)[:, None]
    t_idx = jnp.arange(S5_T)[None, :]
    lag = jnp.where(t_idx >= s_idx, t_idx - s_idx, S5_T)
    w_toep = taps[lag]
    w_toep = w_toep.transpose(2, 0, 4, 1, 3).reshape(N_GROUPS, S5_T * GROUP, S5_T * GROUP)

    rev_re = pw_re[S5_T - 1::-1][:S5_T]
    rev_im = pw_im[S5_T - 1::-1][:S5_T]
    in_re = rev_re[..., None] * bb_re[None] - rev_im[..., None] * bb_im[None]
    in_im = rev_re[..., None] * bb_im[None] + rev_im[..., None] * bb_re[None]
    in_re = in_re.transpose(1, 0, 3, 2).reshape(N_GROUPS, S5_T * GROUP, P_STATE)
    in_im = in_im.transpose(1, 0, 3, 2).reshape(N_GROUPS, S5_T * GROUP, P_STATE)
    w_in = jnp.concatenate([in_re, in_im, in_im, in_re], axis=-1)

    out_re = cl_re[1:].transpose(1, 3, 0, 2).reshape(N_GROUPS, P_STATE, S5_T * GROUP)
    out_im = -cl_im[1:].transpose(1, 3, 0, 2).reshape(N_GROUPS, P_STATE, S5_T * GROUP)
    w_out = jnp.concatenate([out_re, out_im], axis=1)

    lt_re, lt_im = pw_re[S5_T], pw_im[S5_T]
    zeros = jnp.zeros_like(lt_re)
    coef = jnp.stack([jnp.concatenate([lt_re, lt_re], -1),
                      jnp.concatenate([-lt_im, lt_im], -1),
                      jnp.concatenate([lt_im, -lt_im], -1)]
                     + [jnp.concatenate([zeros, zeros], -1)] * 5, axis=1)
    return w_toep.astype(BF16), w_in.astype(BF16), w_out.astype(BF16), coef


def _s5_body(a_ref, wt_ref, win_ref, wout_ref, coef_ref, h0_ref, y_ref, hfin_ref, s_ref, hs_ref, *, nb, n_chunks):
    two_p = 2 * P_STATE
    a = a_ref[...]
    s_ref[...] = _dot(a, win_ref[...])
    coef = coef_ref[...]
    c_self = coef[0:1, :]
    c_swap = coef[1:2, :]
    c_swap_t = coef[2:3, :]
    h0 = h0_ref[...]

    def step(k, carry):
        h, ht = carry
        rows = pl.ds(pl.multiple_of(k * nb, nb), nb)
        hs_ref[rows, :] = h.astype(BF16)
        s = s_ref[rows, :]
        h_new = c_self * h + c_swap * ht + s[:, :two_p]
        ht_new = c_self * ht + c_swap_t * h + s[:, two_p:]
        return h_new, ht_new

    h, _ = lax.fori_loop(0, n_chunks, step, (h0[:, :two_p], h0[:, two_p:]))
    hfin_ref[...] = h
    y_ref[...] = _dot(a, wt_ref[...]) + _dot(hs_ref[...], wout_ref[...])


def _s5_core(u, h0_re, h0_im, weights):
    w_toep, w_in, w_out, coef = weights
    nb, length, _ = u.shape
    n_chunks = length // S5_T
    rows = n_chunks * nb
    tc = S5_T * GROUP
    a = u.astype(BF16).reshape(nb, n_chunks, S5_T, N_GROUPS, GROUP)
    a = a.transpose(3, 1, 0, 2, 4).reshape(N_GROUPS, rows, tc)
    hr = h0_re.astype(F32).transpose(1, 0, 2)
    hi = h0_im.astype(F32).transpose(1, 0, 2)
    h0 = jnp.concatenate([hr, hi, hi, hr], axis=-1)

    def per_group(shape):
        return pl.BlockSpec((None,) + shape, lambda g: (g,) + (0,) * len(shape))

    y, hfin = pl.pallas_call(
        functools.partial(_s5_body, nb=nb, n_chunks=n_chunks),
        grid=(N_GROUPS,),
        in_specs=[per_group((rows, tc)), per_group((tc, tc)), per_group((tc, 4 * P_STATE)),
                  per_group((2 * P_STATE, tc)), per_group((8, 2 * P_STATE)), per_group((nb, 4 * P_STATE))],
        out_specs=[per_group((rows, tc)), per_group((nb, 2 * P_STATE))],
        out_shape=[jax.ShapeDtypeStruct((N_GROUPS, rows, tc), F32),
                   jax.ShapeDtypeStruct((N_GROUPS, nb, 2 * P_STATE), F32)],
        scratch_shapes=[pltpu.VMEM((rows, 4 * P_STATE), F32), pltpu.VMEM((rows, 2 * P_STATE), BF16)],
        compiler_params=_params(1),
        name="s5_core",
    )(a, w_toep, w_in, w_out, coef, h0)
    y = y.reshape(N_GROUPS, n_chunks, nb, S5_T, GROUP).transpose(2, 1, 3, 0, 4).reshape(nb, length, D_MODEL)
    hfin = hfin.transpose(1, 0, 2)
    return y, hfin[..., :P_STATE], hfin[..., P_STATE:]


def _glu_ln_body(x_ref, ys_ref, d_ref, w_ref, g_ref, b_ref, o_ref):
    x = x_ref[...]
    z = _gelu_tanh(ys_ref[...] + d_ref[...] * x).astype(BF16)
    out = _dot(z, w_ref[:, pl.ds(0, D_MODEL)])
    gate = _dot(z, w_ref[:, pl.ds(D_MODEL, D_MODEL)])
    o_ref[...] = _layer_norm(ALPHA * x + out * _sigmoid(gate), g_ref[...], b_ref[...])


def _glu_ln(x, ys, d_skip, w_glu, g, b):
    n = x.shape[0]
    tm = _row_tile(n)
    row = pl.BlockSpec((tm, D_MODEL), lambda i: (i, 0))
    vec = _const_spec((1, D_MODEL))
    return pl.pallas_call(
        _glu_ln_body,
        grid=(n // tm,),
        in_specs=[row, row, vec, _const_spec((D_MODEL, 2 * D_MODEL)), vec, vec],
        out_specs=row,
        out_shape=jax.ShapeDtypeStruct((n, D_MODEL), F32),
        compiler_params=_params(1),
        name="glu_ln",
    )(x, ys, d_skip, w_glu, g, b)


def _rope_tables(pos, rows):
    half = HEAD_DIM // 2
    inv = ROPE_THETA ** (-jnp.arange(half, dtype=F32) / half)
    ang = pos.astype(F32)[:, None] * inv[None, :]
    cos = jnp.cos(ang)
    sin = jnp.sin(ang)
    cos_t = jnp.concatenate([cos, cos, cos, cos], axis=-1)
    sin_t = jnp.concatenate([-sin, sin, -sin, sin], axis=-1)
    reps = rows // pos.shape[0]
    return jnp.tile(cos_t, (reps, 1)), jnp.tile(sin_t, (reps, 1))


def _rope_slab(v, cos, sin, first_half):
    partner = jnp.where(first_half, pltpu.roll(v, LANES - HEAD_DIM // 2, 1), pltpu.roll(v, HEAD_DIM // 2, 1))
    return v * cos + partner * sin


def _first_half_mask(rows):
    lane = lax.broadcasted_iota(jnp.int32, (rows, LANES), 1)
    return (lane % HEAD_DIM) < (HEAD_DIM // 2)


def _kv_rope_body(x_ref, w_ref, cos_ref, sin_ref, k_ref, v_ref, kb_ref, vb_ref):
    xb = x_ref[...].astype(BF16)
    cos = cos_ref[...]
    sin = sin_ref[...]
    first_half = _first_half_mask(xb.shape[0])
    for j in range(D_MODEL // LANES):
        cols = pl.ds(j * LANES, LANES)
        k = _rope_slab(_dot(xb, w_ref[:, cols]), cos, sin, first_half)
        k_ref[:, cols] = k
        kb_ref[:, cols] = k.astype(BF16)
    v = _dot(xb, w_ref[:, pl.ds(D_MODEL, D_MODEL)])
    v_ref[...] = v
    vb_ref[...] = v.astype(BF16)


def _q_rope_body(x_ref, w_ref, cos_ref, sin_ref, q_ref):
    xb = x_ref[...].astype(BF16)
    cos = cos_ref[...]
    sin = sin_ref[...]
    first_half = _first_half_mask(xb.shape[0])
    scale = HEAD_DIM ** -0.5
    for j in range(D_MODEL // LANES):
        cols = pl.ds(j * LANES, LANES)
        q = _rope_slab(_dot(xb, w_ref[:, cols]), cos, sin, first_half)
        q_ref[:, cols] = (q * scale).astype(BF16)


def _rope_call(body, x, w, pos, length, out_dtypes, name):
    n = x.shape[0]
    tm = _row_tile(n)
    table_rows = max(length, tm)
    assert table_rows % length == 0 and table_rows % tm == 0
    cos_t, sin_t = _rope_tables(pos, table_rows)
    n_table_tiles = table_rows // tm
    row = pl.BlockSpec((tm, D_MODEL), lambda i: (i, 0))
    table = pl.BlockSpec((tm, LANES), lambda i: (i % n_table_tiles, 0))
    return pl.pallas_call(
        body,
        grid=(n // tm,),
        in_specs=[row, _const_spec(w.shape), table, table],
        out_specs=[row] * len(out_dtypes),
        out_shape=[jax.ShapeDtypeStruct((n, D_MODEL), dt) for dt in out_dtypes],
        compiler_params=_params(1),
        name=name,
    )(x, w, cos_t, sin_t)


def _lam_value(lam_ref, layer_idx):
    lv = lam_ref[...]
    s01 = jnp.sum(lv[0:1, :] * lv[1:2, :], axis=-1, keepdims=True)
    s23 = jnp.sum(lv[2:3, :] * lv[3:4, :], axis=-1, keepdims=True)
    lam_init = 0.8 - 0.6 * math.exp(-0.3 * layer_idx)
    return jnp.exp(s01) - jnp.exp(s23) + lam_init, lam_init


def _diff_attend(q, segments, lam, lam_init, subln_g):
    lane = lax.broadcasted_iota(jnp.int32, q.shape, 1)
    zero = jnp.zeros_like(q)
    q1 = jnp.where(lane < HEAD_DIM, q, zero)
    q2 = jnp.where(lane >= HEAD_DIM, q, zero)
    probs = []
    for qm in (q1, q2):
        scores = []
        for k, _, mask in segments:
            s = _dot_nt(qm, k)
            scores.append(s if mask is None else jnp.where(mask, s, NEG_INF))
        m = functools.reduce(jnp.maximum, [jnp.max(s, axis=-1, keepdims=True) for s in scores])
        es = [jnp.exp(s - m) for s in scores]
        denom = functools.reduce(jnp.add, [jnp.sum(e, axis=-1, keepdims=True) for e in es])
        probs.append((es, denom))
    (e1, l1), (e2, l2) = probs
    w1 = 1.0 / l1
    w2 = lam / l2
    out = None
    for i, (_, v, _) in enumerate(segments):
        p = (e1[i] * w1 - e2[i] * w2).astype(BF16)
        part = _dot(p, v)
        out = part if out is None else out + part
    ms = jnp.mean(out * out, axis=-1, keepdims=True)
    return out * lax.rsqrt(ms + RMS_EPS) * subln_g * (1.0 - lam_init)


def _attn_prompt_body(q_ref, k_ref, v_ref, lam_ref, g_ref, o_ref, *, length, tq, layer_idx):
    lam, lam_init = _lam_value(lam_ref, layer_idx)
    g = g_ref[...]
    r = lax.broadcasted_iota(jnp.int32, (tq, tq), 0) // CHUNK
    c = lax.broadcasted_iota(jnp.int32, (tq, tq), 1) // CHUNK
    diag_mask = c <= r
    for i in range(length // tq):
        rows = pl.ds(i * tq, tq)
        segments = []
        if i > 0:
            segments.append((k_ref[pl.ds(0, i * tq), :], v_ref[pl.ds(0, i * tq), :], None))
        segments.append((k_ref[rows, :], v_ref[rows, :], diag_mask))
        o_ref[rows, :] = _diff_attend(q_ref[rows, :], segments, lam, lam_init, g).astype(BF16)


def _attn_prompt(q, kb, vb, attn_lam, subln_g, layer_idx):
    nb, length, _ = q.shape
    tq = min(Q_TILE, length)
    head = pl.BlockSpec((None, length, V_DIM), lambda b, h: (b, 0, h))
    return pl.pallas_call(
        functools.partial(_attn_prompt_body, length=length, tq=tq, layer_idx=layer_idx),
        grid=(nb, N_HEADS),
        in_specs=[head, head, head, _const_spec((4, HEAD_DIM)), _const_spec((1, V_DIM))],
        out_specs=head,
        out_shape=jax.ShapeDtypeStruct((nb, length, D_MODEL), BF16),
        compiler_params=_params(2),
        name="attn_prompt",
    )(q, kb, vb, attn_lam, subln_g)


def _attn_sample_body(q_ref, kp_ref, vp_ref, k_ref, v_ref, lam_ref, g_ref, o_ref, *, layer_idx):
    lam, lam_init = _lam_value(lam_ref, layer_idx)
    segments = [(kp_ref[...].astype(BF16), vp_ref[...].astype(BF16), None), (k_ref[...], v_ref[...], None)]
    o_ref[...] = _diff_attend(q_ref[...], segments, lam, lam_init, g_ref[...]).astype(BF16)


def _attn_sample(q, cache_k, cache_v, kb, vb, attn_lam, subln_g, layer_idx):
    nb, length, _ = q.shape
    past = cache_k.shape[1]
    assert past % CHUNK == 0 and length <= CHUNK
    head = pl.BlockSpec((None, length, V_DIM), lambda b, h: (b, 0, h))
    past_head = pl.BlockSpec((None, past, V_DIM), lambda b, h: (b, 0, h))
    return pl.pallas_call(
        functools.partial(_attn_sample_body, layer_idx=layer_idx),
        grid=(nb, N_HEADS),
        in_specs=[head, past_head, past_head, head, head, _const_spec((4, HEAD_DIM)), _const_spec((1, V_DIM))],
        out_specs=head,
        out_shape=jax.ShapeDtypeStruct((nb, length, D_MODEL), BF16),
        compiler_params=_params(2),
        name="attn_sample",
    )(q, cache_k.reshape(nb, past, D_MODEL), cache_v.reshape(nb, past, D_MODEL), kb, vb, attn_lam, subln_g)


def _proj_ln_body(x_ref, o_ref_in, w_ref, g_ref, b_ref, out_ref):
    y = _dot(o_ref_in[...], w_ref[...])
    out_ref[...] = _layer_norm(ALPHA * x_ref[...] + y, g_ref[...], b_ref[...])


def _proj_ln(x, o, w_o, g, b):
    n = x.shape[0]
    tm = _row_tile(n)
    row = pl.BlockSpec((tm, D_MODEL), lambda i: (i, 0))
    vec = _const_spec((1, D_MODEL))
    return pl.pallas_call(
        _proj_ln_body,
        grid=(n // tm,),
        in_specs=[row, row, _const_spec((D_MODEL, D_MODEL)), vec, vec],
        out_specs=row,
        out_shape=jax.ShapeDtypeStruct((n, D_MODEL), F32),
        compiler_params=_params(1),
        name="proj_ln",
    )(x, o, w_o, g, b)


def _run_trunk(x, pos, h0_re, h0_im, cache_k, cache_v, w):
    nb, length, _ = x.shape
    n = nb * length
    assert length % S5_T == 0
    ln_g, ln_b = w['ln_g'], w['ln_b']

    def vec(v):
        return v.reshape(1, D_MODEL).astype(F32)

    def ffn(t, l, j, ln_idx):
        return _ffn_ln(t, w['ffn_w_in'][l, j], w['ffn_w_out'][l, j], vec(ln_g[l, ln_idx]), vec(ln_b[l, ln_idx]))

    t = x.reshape(n, D_MODEL)
    new_re, new_im = [], []
    k = v = None
    for l in range(DEPTH):
        if l == N_A:
            k, v, kb, vb = _rope_call(_kv_rope_body, t, w['attn_w_kv'], pos, length, (F32, F32, BF16, BF16), "kv_rope")
        t = ffn(t, l, 0, 0)
        if l < N_A:
            ys, hr, hi = _s5_core(t.reshape(nb, length, D_MODEL), h0_re[l], h0_im[l], w['ssm'][l])
            new_re.append(hr)
            new_im.append(hi)
            t = _glu_ln(t, ys.reshape(n, D_MODEL), vec(w['ssm_d'][l]), w['ssm_w_glu'][l],
                        vec(ln_g[l, 1]), vec(ln_b[l, 1]))
        else:
            j = l - N_A
            (q,) = _rope_call(_q_rope_body, t, w['attn_w_q'][j], pos, length, (BF16,), "q_rope")
            q = q.reshape(nb, length, D_MODEL)
            kb3 = kb.reshape(nb, length, D_MODEL)
            vb3 = vb.reshape(nb, length, D_MODEL)
            lam_p = w['attn_lam'][j].astype(F32)
            sub_g = w['attn_subln_g'][j].reshape(1, V_DIM).astype(F32)
            if cache_k is None:
                o = _attn_prompt(q, kb3, vb3, lam_p, sub_g, l)
            else:
                o = _attn_sample(q, cache_k, cache_v, kb3, vb3, lam_p, sub_g, l)
            t = _proj_ln(t, o.reshape(n, D_MODEL), w['attn_w_o'][j], vec(ln_g[l, 1]), vec(ln_b[l, 1]))
        t = ffn(t, l, 1, 2)
    y = t.reshape(nb, length, D_MODEL)
    p_k = k.reshape(nb, length, N_HEADS, 2, HEAD_DIM)
    p_v = v.reshape(nb, length, N_HEADS, V_DIM)
    return y, jnp.stack(new_re), jnp.stack(new_im), p_k, p_v


def kernel(x_prompt, x_sample, cache_k, cache_v, state_ssm_re, state_ssm_im,
           ln_g, ln_b, ffn_w_in, ffn_w_out,
           ssm_a_re, ssm_a_im, ssm_log_dt, ssm_b_re, ssm_b_im, ssm_c_re, ssm_c_im, ssm_d, ssm_w_glu,
           attn_w_q, attn_w_kv, attn_lam, attn_subln_g, attn_w_o):
    w = {'ln_g': ln_g, 'ln_b': ln_b,
         'ffn_w_in': ffn_w_in.astype(BF16), 'ffn_w_out': ffn_w_out.astype(BF16),
         'ssm': [_s5_weights(ssm_a_re[l], ssm_a_im[l], ssm_log_dt[l], ssm_b_re[l], ssm_b_im[l],
                             ssm_c_re[l], ssm_c_im[l]) for l in range(N_A)],
         'ssm_d': ssm_d, 'ssm_w_glu': ssm_w_glu.astype(BF16),
         'attn_w_q': attn_w_q.astype(BF16), 'attn_w_kv': attn_w_kv.astype(BF16), 'attn_lam': attn_lam,
         'attn_subln_g': attn_subln_g, 'attn_w_o': attn_w_o.astype(BF16)}
    b_p, l_p, _ = x_prompt.shape
    pos_p = jnp.arange(l_p, dtype=jnp.int32)
    h0 = jnp.zeros((N_A, b_p, N_GROUPS, P_STATE), F32)
    y_p, p_re, p_im, p_k, p_v = _run_trunk(x_prompt, pos_p, h0, h0, None, None, w)
    past = cache_k.shape[1]
    pos_s = past + jnp.arange(x_sample.shape[1], dtype=jnp.int32)
    y_s, s_re, s_im, s_k, s_v = _run_trunk(x_sample, pos_s, state_ssm_re, state_ssm_im, cache_k, cache_v, w)
    return (y_p, y_s, p_re, p_im, p_k, p_v, s_re, s_im, s_k, s_v)
```

```python
import functools
import math

import jax
import jax.numpy as jnp
from jax import lax
from jax.experimental import pallas as pl
from jax.experimental.pallas import tpu as pltpu

F32 = jnp.float32
BF16 = jnp.bfloat16

D_MODEL = 1024
DEPTH = 2
CHUNK = 64
N_A = DEPTH // 2
GROUP = 16
N_GROUPS = D_MODEL // GROUP
P_STATE = 64
HEAD_DIM = 64
V_DIM = 2 * HEAD_DIM
N_HEADS = D_MODEL // V_DIM
D_FF = ((8 * D_MODEL // 3 + 127) // 128) * 128
ROPE_THETA = 10000.0
ALPHA = (2 * DEPTH) ** 0.25
LN_EPS = 1e-5
RMS_EPS = 1e-5
NEG_INF = -1e30

LANES = 128
ROW_TILE = 512
FF_TILE = 256
S5_T = 16
SLAB_GROUPS = LANES // GROUP
N_SLABS = N_GROUPS // SLAB_GROUPS
S5_ROWS = 512
Q_TILE = 256
PAST_TILE = 1024
VMEM_LIMIT = 56 * 1024 * 1024


def _params(n_axes):
    return pltpu.CompilerParams(dimension_semantics=("arbitrary",) * n_axes, vmem_limit_bytes=VMEM_LIMIT)


def _row_tile(n):
    return ROW_TILE if n % ROW_TILE == 0 else n


def _const_spec(shape):
    return pl.BlockSpec(shape, lambda *_: (0,) * len(shape), pipeline_mode=pl.Buffered(1))


def _layer_norm(v, g, b):
    mu = jnp.mean(v, axis=-1, keepdims=True)
    c = v - mu
    var = jnp.mean(c * c, axis=-1, keepdims=True)
    return c * lax.rsqrt(var + LN_EPS) * g + b


def _sigmoid(v):
    return 0.5 * jnp.tanh(0.5 * v) + 0.5


def _gelu_tanh(v):
    return 0.5 * v * (1.0 + jnp.tanh(math.sqrt(2.0 / math.pi) * (v + 0.044715 * (v * v * v))))


def _dot(a, b):
    return jnp.dot(a, b, preferred_element_type=F32)


def _dot_nt(a, b):
    return lax.dot_general(a, b, (((1,), (1,)), ((), ())), preferred_element_type=F32)


def _ffn_ln_body(x_ref, win_ref, wout_ref, g_ref, b_ref, o_ref):
    x = x_ref[...]
    xb = x.astype(BF16)
    acc = None
    for c in range(D_FF // FF_TILE):
        a = _dot(xb, win_ref[:, pl.ds(c * FF_TILE, FF_TILE)])
        gate = _dot(xb, win_ref[:, pl.ds(D_FF + c * FF_TILE, FF_TILE)])
        h = (a * _sigmoid(a) * gate).astype(BF16)
        part = _dot(h, wout_ref[pl.ds(c * FF_TILE, FF_TILE), :])
        acc = part if acc is None else acc + part
    o_ref[...] = _layer_norm(ALPHA * x + 0.5 * acc, g_ref[...], b_ref[...])


def _ffn_ln(x, w_in, w_out, g, b):
    n = x.shape[0]
    tm = _row_tile(n)
    row = pl.BlockSpec((tm, D_MODEL), lambda i: (i, 0))
    return pl.pallas_call(
        _ffn_ln_body,
        grid=(n // tm,),
        in_specs=[row, _const_spec((D_MODEL, 2 * D_FF)), _const_spec((D_FF, D_MODEL)),
                  _const_spec((1, D_MODEL)), _const_spec((1, D_MODEL))],
        out_specs=row,
        out_shape=jax.ShapeDtypeStruct((n, D_MODEL), F32),
        compiler_params=_params(1),
        name="ffn_ln",
    )(x, w_in, w_out, g, b)


def _s5_weights(a_re, a_im, log_dt, b_re, b_im, c_re, c_im):
    hi = lax.Precision.HIGHEST
    lam_re = jnp.minimum(a_re.astype(F32), -1e-4)
    lam_im = a_im.astype(F32)
    dt = jnp.exp(log_dt.astype(F32))[:, None]
    mag = jnp.exp(lam_re * dt)
    lb_re = mag * jnp.cos(lam_im * dt)
    lb_im = mag * jnp.sin(lam_im * dt)
    den = lam_re * lam_re + lam_im * lam_im
    n_re = lb_re - 1.0
    k_re = (n_re * lam_re + lb_im * lam_im) / den
    k_im = (lb_im * lam_re - n_re * lam_im) / den
    b_re = b_re.astype(F32)
    b_im = b_im.astype(F32)
    bb_re = k_re[..., None] * b_re - k_im[..., None] * b_im
    bb_im = k_re[..., None] * b_im + k_im[..., None] * b_re
    c_re = c_re.astype(F32)
    c_im = c_im.astype(F32)

    def power_step(carry, _):
        pr, pi = carry
        return (pr * lb_re - pi * lb_im, pr * lb_im + pi * lb_re), (pr, pi)

    _, (pw_re, pw_im) = lax.scan(power_step, (jnp.ones_like(lb_re), jnp.zeros_like(lb_re)), None,
                                 length=S5_T + 1)

    cl_re = c_re[None] * pw_re[:, :, None, :] - c_im[None] * pw_im[:, :, None, :]
    cl_im = c_re[None] * pw_im[:, :, None, :] + c_im[None] * pw_re[:, :, None, :]

    taps = (jnp.einsum('ngdp,gpc->ngdc', cl_re[:S5_T], bb_re, precision=hi)
            - jnp.einsum('ngdp,gpc->ngdc', cl_im[:S5_T], bb_im, precision=hi))
    taps = jnp.concatenate([taps, jnp.zeros_like(taps[:1])], axis=0)
    s_idx = jnp.arange(S5_T)[:, None]
    t_idx = jnp.arange(S5_T)[None, :]
    lag = jnp.where(t_idx >= s_idx, t_idx - s_idx, S5_T)
    wide = S5_T * LANES
    state_w = SLAB_GROUPS * 2 * P_STATE

    def slab_diag(v, j_axis, new_axis):
        shape = v.shape[:new_axis] + (SLAB_GROUPS,) + v.shape[new_axis:]
        out = jnp.zeros(shape, BF16)
        for j in range(SLAB_GROUPS):
            src = [slice(None)] * v.ndim
            src[j_axis] = j
            dst = [slice(None)] * len(shape)
            dst[j_axis] = j
            dst[new_axis] = j
            out = out.at[tuple(dst)].set(v[tuple(src)].astype(BF16))
        return out

    toep = taps[lag].reshape(S5_T, S5_T, N_SLABS, SLAB_GROUPS, GROUP, GROUP)
    toep = toep.transpose(2, 0, 3, 5, 1, 4)
    w_toep = slab_diag(toep, 2, 5).reshape(N_SLABS, wide, wide)

    rev_re = pw_re[S5_T - 1::-1][:S5_T]
    rev_im = pw_im[S5_T - 1::-1][:S5_T]
    in_re = rev_re[..., None] * bb_re[None] - rev_im[..., None] * bb_im[None]
    in_im = rev_re[..., None] * bb_im[None] + rev_im[..., None] * bb_re[None]
    w_in = jnp.concatenate([in_re, in_im], axis=2)
    w_in = w_in.reshape(S5_T, N_SLABS, SLAB_GROUPS, 2 * P_STATE, GROUP).transpose(1, 0, 2, 4, 3)
    w_in = slab_diag(w_in, 2, 4).reshape(N_SLABS, wide, state_w)

    w_out = jnp.concatenate([cl_re[1:], -cl_im[1:]], axis=3)
    w_out = w_out.reshape(S5_T, N_SLABS, SLAB_GROUPS, GROUP, 2 * P_STATE).transpose(1, 2, 4, 0, 3)
    w_out = slab_diag(w_out, 1, 4).reshape(N_SLABS, state_w, wide)

    lt_re, lt_im = pw_re[S5_T], pw_im[S5_T]
    coef = jnp.stack([jnp.concatenate([lt_re, lt_re], -1),
                      jnp.concatenate([-lt_im, lt_im], -1),
                      jnp.concatenate([lt_im, -lt_im], -1)], axis=0)
    coef = coef.reshape(3, N_SLABS, state_w).transpose(1, 0, 2)
    coef = jnp.concatenate([coef, jnp.zeros((N_SLABS, 5, state_w), F32)], axis=1)
    return w_toep, w_in, w_out, coef


def _s5_body(x_ref, wt_ref, win_ref, wout_ref, coef_ref, h0_ref, y_ref, hfin_ref, s_ref, hst_ref, *, rows, n_chunks, bt):
    xcat = jnp.concatenate([x_ref[pl.ds(s, rows, stride=S5_T), :].astype(BF16) for s in range(S5_T)], axis=1)
    s_all = _dot(xcat, win_ref[...])
    for j in range(SLAB_GROUPS):
        sj = s_all[:, j * LANES:(j + 1) * LANES]
        s_ref[j] = sj
        s_ref[SLAB_GROUPS + j] = pltpu.roll(sj, P_STATE, 1)
    coef = coef_ref[...]
    h0 = h0_ref[...]

    def lanes(v, j):
        return v[:, j * LANES:(j + 1) * LANES]

    def step(k, carry):
        hs, hts = carry
        idx = pl.ds(k, bt, stride=n_chunks)
        new_h, new_ht = [], []
        for j in range(SLAB_GROUPS):
            hst_ref[j, idx, :] = hs[j]
            c_self, c_swap, c_swap_t = lanes(coef[0:1], j), lanes(coef[1:2], j), lanes(coef[2:3], j)
            new_h.append(c_self * hs[j] + c_swap * hts[j] + s_ref[j, idx, :])
            new_ht.append(c_self * hts[j] + c_swap_t * hs[j] + s_ref[SLAB_GROUPS + j, idx, :])
        return tuple(new_h), tuple(new_ht)

    init = (tuple(lanes(h0[0], j) for j in range(SLAB_GROUPS)), tuple(lanes(h0[1], j) for j in range(SLAB_GROUPS)))
    hs, _ = lax.fori_loop(0, n_chunks, step, init)
    hfin_ref[...] = jnp.concatenate(hs, axis=1)
    hst = jnp.concatenate([hst_ref[j] for j in range(SLAB_GROUPS)], axis=1).astype(BF16)
    y = _dot(xcat, wt_ref[...]) + _dot(hst, wout_ref[...])
    for t in range(S5_T):
        y_ref[pl.ds(t, rows, stride=S5_T), :] = y[:, t * LANES:(t + 1) * LANES]


def _s5_batch_tile(nb, n_chunks):
    bt = nb
    while bt * n_chunks > S5_ROWS and bt % 2 == 0:
        bt //= 2
    return bt


def _s5_core(u, nb, h0_re, h0_im, weights):
    w_toep, w_in, w_out, coef = weights
    n = u.shape[0]
    length = n // nb
    n_chunks = length // S5_T
    bt = _s5_batch_tile(nb, n_chunks)
    n_bt = nb // bt
    rows = bt * n_chunks
    wide = S5_T * LANES
    sw = SLAB_GROUPS * 2 * P_STATE

    def slab_state(re, im):
        v = jnp.concatenate([re, im], axis=-1).astype(F32)
        return v.reshape(n_bt, bt, N_SLABS, sw).transpose(2, 0, 1, 3)

    h0 = jnp.stack([slab_state(h0_re, h0_im), slab_state(h0_im, h0_re)], axis=2)

    def per_slab(shape):
        return pl.BlockSpec((None,) + shape, lambda s, i: (s,) + (0,) * len(shape), pipeline_mode=pl.Buffered(1))

    tokens = pl.BlockSpec((bt * length, LANES), lambda s, i: (i, s))
    y, hfin = pl.pallas_call(
        functools.partial(_s5_body, rows=rows, n_chunks=n_chunks, bt=bt),
        grid=(N_SLABS, n_bt),
        in_specs=[tokens, per_slab((wide, wide)), per_slab((wide, sw)), per_slab((sw, wide)), per_slab((8, sw)),
                  pl.BlockSpec((None, None, 2, bt, sw), lambda s, i: (s, i, 0, 0, 0))],
        out_specs=[tokens, pl.BlockSpec((None, None, bt, sw), lambda s, i: (s, i, 0, 0))],
        out_shape=[jax.ShapeDtypeStruct((n, D_MODEL), F32), jax.ShapeDtypeStruct((N_SLABS, n_bt, bt, sw), F32)],
        scratch_shapes=[pltpu.VMEM((2 * SLAB_GROUPS, rows, LANES), F32), pltpu.VMEM((SLAB_GROUPS, rows, LANES), F32)],
        compiler_params=_params(2),
        name="s5_core",
    )(u, w_toep, w_in, w_out, coef, h0)
    hfin = hfin.transpose(1, 2, 0, 3).reshape(nb, N_GROUPS, 2 * P_STATE)
    return y, hfin[..., :P_STATE], hfin[..., P_STATE:]


def _glu_ln_body(x_ref, ys_ref, d_ref, w_ref, g_ref, b_ref, o_ref):
    x = x_ref[...]
    z = _gelu_tanh(ys_ref[...] + d_ref[...] * x).astype(BF16)
    out = _dot(z, w_ref[:, pl.ds(0, D_MODEL)])
    gate = _dot(z, w_ref[:, pl.ds(D_MODEL, D_MODEL)])
    o_ref[...] = _layer_norm(ALPHA * x + out * _sigmoid(gate), g_ref[...], b_ref[...])


def _glu_ln(x, ys, d_skip, w_glu, g, b):
    n = x.shape[0]
    tm = _row_tile(n)
    row = pl.BlockSpec((tm, D_MODEL), lambda i: (i, 0))
    vec = _const_spec((1, D_MODEL))
    return pl.pallas_call(
        _glu_ln_body,
        grid=(n // tm,),
        in_specs=[row, row, vec, _const_spec((D_MODEL, 2 * D_MODEL)), vec, vec],
        out_specs=row,
        out_shape=jax.ShapeDtypeStruct((n, D_MODEL), F32),
        compiler_params=_params(1),
        name="glu_ln",
    )(x, ys, d_skip, w_glu, g, b)


def _rope_tables(pos, rows):
    half = HEAD_DIM // 2
    inv = ROPE_THETA ** (-jnp.arange(half, dtype=F32) / half)
    ang = pos.astype(F32)[:, None] * inv[None, :]
    cos = jnp.cos(ang)
    sin = jnp.sin(ang)
    cos_t = jnp.concatenate([cos, cos, cos, cos], axis=-1)
    sin_t = jnp.concatenate([-sin, sin, -sin, sin], axis=-1)
    reps = rows // pos.shape[0]
    return jnp.tile(cos_t, (reps, 1)), jnp.tile(sin_t, (reps, 1))


def _rope_slab(v, cos, sin, first_half):
    partner = jnp.where(first_half, pltpu.roll(v, LANES - HEAD_DIM // 2, 1), pltpu.roll(v, HEAD_DIM // 2, 1))
    return v * cos + partner * sin


def _first_half_mask(rows):
    lane = lax.broadcasted_iota(jnp.int32, (rows, LANES), 1)
    return (lane % HEAD_DIM) < (HEAD_DIM // 2)


def _kv_rope_body(x_ref, w_ref, cos_ref, sin_ref, k_ref, v_ref, kb_ref, vb_ref):
    xb = x_ref[...].astype(BF16)
    cos = cos_ref[...]
    sin = sin_ref[...]
    tm = xb.shape[0]
    first_half = _first_half_mask(tm)
    for h in range(N_HEADS):
        cols = pl.ds(h * V_DIM, V_DIM)
        k = _rope_slab(_dot(xb, w_ref[:, cols]), cos, sin, first_half)
        k_ref[pl.ds(2 * h, tm, stride=2 * N_HEADS), :] = k[:, :HEAD_DIM]
        k_ref[pl.ds(2 * h + 1, tm, stride=2 * N_HEADS), :] = k[:, HEAD_DIM:]
        kb_ref[:, cols] = k.astype(BF16)
        v = _dot(xb, w_ref[:, pl.ds(D_MODEL + h * V_DIM, V_DIM)])
        v_ref[pl.ds(h, tm, stride=N_HEADS), :] = v
        vb_ref[:, cols] = v.astype(BF16)


def _q_rope_body(x_ref, w_ref, cos_ref, sin_ref, q_ref):
    xb = x_ref[...].astype(BF16)
    cos = cos_ref[...]
    sin = sin_ref[...]
    first_half = _first_half_mask(xb.shape[0])
    scale = HEAD_DIM ** -0.5
    for j in range(D_MODEL // LANES):
        cols = pl.ds(j * LANES, LANES)
        q = _rope_slab(_dot(xb, w_ref[:, cols]), cos, sin, first_half)
        q_ref[:, cols] = (q * scale).astype(BF16)


def _rope_call(body, x, w, pos, length, outs, name):
    n = x.shape[0]
    tm = _row_tile(n)
    table_rows = max(length, tm)
    assert table_rows % length == 0 and table_rows % tm == 0
    cos_t, sin_t = _rope_tables(pos, table_rows)
    n_table_tiles = table_rows // tm

    def rows(per_token, trailing):
        zeros = (0,) * len(trailing)
        return pl.BlockSpec((tm * per_token,) + trailing, lambda i: (i,) + zeros)

    table = pl.BlockSpec((tm, LANES), lambda i: (i % n_table_tiles, 0))
    return pl.pallas_call(
        body,
        grid=(n // tm,),
        in_specs=[rows(1, (D_MODEL,)), _const_spec(w.shape), table, table],
        out_specs=[rows(per_token, trailing) for per_token, trailing, _ in outs],
        out_shape=[jax.ShapeDtypeStruct((n * per_token,) + trailing, dt) for per_token, trailing, dt in outs],
        compiler_params=_params(1),
        name=name,
    )(x, w, cos_t, sin_t)


def _lam_value(lam_ref, layer_idx):
    lv = lam_ref[...]
    s01 = jnp.sum(lv[0:1, :] * lv[1:2, :], axis=-1, keepdims=True)
    s23 = jnp.sum(lv[2:3, :] * lv[3:4, :], axis=-1, keepdims=True)
    lam_init = 0.8 - 0.6 * math.exp(-0.3 * layer_idx)
    return jnp.exp(s01) - jnp.exp(s23) + lam_init, lam_init


def _diff_attend(q, segments, lam, lam_init, subln_g):
    lane = lax.broadcasted_iota(jnp.int32, q.shape, 1)
    zero = jnp.zeros_like(q)
    q1 = jnp.where(lane < HEAD_DIM, q, zero)
    q2 = jnp.where(lane >= HEAD_DIM, q, zero)
    probs = []
    for qm in (q1, q2):
        scores = []
        for k, _, mask in segments:
            s = _dot_nt(qm, k)
            scores.append(s if mask is None else jnp.where(mask, s, NEG_INF))
        m = functools.reduce(jnp.maximum, [jnp.max(s, axis=-1, keepdims=True) for s in scores])
        es = [jnp.exp(s - m) for s in scores]
        denom = functools.reduce(jnp.add, [jnp.sum(e, axis=-1, keepdims=True) for e in es])
        probs.append((es, denom))
    (e1, l1), (e2, l2) = probs
    w1 = 1.0 / l1
    w2 = lam / l2
    out = None
    for i, (_, v, _) in enumerate(segments):
        p = (e1[i] * w1 - e2[i] * w2).astype(BF16)
        part = _dot(p, v)
        out = part if out is None else out + part
    ms = jnp.mean(out * out, axis=-1, keepdims=True)
    return out * lax.rsqrt(ms + RMS_EPS) * subln_g * (1.0 - lam_init)


def _attn_prompt_body(q_ref, k_ref, v_ref, lam_ref, g_ref, o_ref, *, length, tq, layer_idx):
    lam, lam_init = _lam_value(lam_ref, layer_idx)
    g = g_ref[...]
    r = lax.broadcasted_iota(jnp.int32, (tq, tq), 0) // CHUNK
    c = lax.broadcasted_iota(jnp.int32, (tq, tq), 1) // CHUNK
    diag_mask = c <= r
    for i in range(length // tq):
        rows = pl.ds(i * tq, tq)
        segments = []
        if i > 0:
            segments.append((k_ref[pl.ds(0, i * tq), :], v_ref[pl.ds(0, i * tq), :], None))
        segments.append((k_ref[rows, :], v_ref[rows, :], diag_mask))
        o_ref[rows, :] = _diff_attend(q_ref[rows, :], segments, lam, lam_init, g).astype(BF16)


def _attn_prompt(q, kb, vb, attn_lam, subln_g, layer_idx):
    nb, length, _ = q.shape
    tq = min(Q_TILE, length)
    head = pl.BlockSpec((None, length, V_DIM), lambda b, h: (b, 0, h))
    return pl.pallas_call(
        functools.partial(_attn_prompt_body, length=length, tq=tq, layer_idx=layer_idx),
        grid=(nb, N_HEADS),
        in_specs=[head, head, head, _const_spec((4, HEAD_DIM)), _const_spec((1, V_DIM))],
        out_specs=head,
        out_shape=jax.ShapeDtypeStruct((nb, length, D_MODEL), BF16),
        compiler_params=_params(2),
        name="attn_prompt",
    )(q, kb, vb, attn_lam, subln_g)


def _attn_sample_body(q_ref, kp_ref, vp_ref, k_ref, v_ref, lam_ref, g_ref, o_ref, m_ref, l_ref, acc_ref, *, layer_idx, tp):
    pt = pl.program_id(1)
    hm = 2 * N_HEADS
    n_q = q_ref.shape[0]
    zero = jnp.zeros((n_q, HEAD_DIM), BF16)

    def head_scores(h, k1, k2):
        q1 = q_ref[:, h * V_DIM:h * V_DIM + HEAD_DIM]
        q2 = q_ref[:, h * V_DIM + HEAD_DIM:(h + 1) * V_DIM]
        return _dot_nt(jnp.concatenate([q1, zero], axis=0), k1) + _dot_nt(jnp.concatenate([zero, q2], axis=0), k2)

    def accumulate(keys_of, values_of, first):
        s = jnp.concatenate([head_scores(h, *keys_of(h)) for h in range(N_HEADS)], axis=0)
        m_tile = jnp.max(s, axis=-1, keepdims=True)
        if first:
            m_new = m_tile
        else:
            m_old = m_ref[...]
            m_new = jnp.maximum(m_old, m_tile)
            alpha = jnp.exp(m_old - m_new)
        e = jnp.exp(s - m_new)
        l_tile = jnp.sum(e, axis=-1, keepdims=True)
        eb = e.astype(BF16)
        pv = jnp.concatenate([_dot(eb[2 * n_q * h:2 * n_q * (h + 1)], values_of(h)) for h in range(N_HEADS)], axis=0)
        m_ref[...] = m_new
        l_ref[...] = l_tile if first else alpha * l_ref[...] + l_tile
        acc_ref[...] = pv if first else alpha * acc_ref[...] + pv

    @pl.when(pt == 0)
    def _():
        accumulate(lambda h: (k_ref[:, h * V_DIM:h * V_DIM + HEAD_DIM], k_ref[:, h * V_DIM + HEAD_DIM:(h + 1) * V_DIM]),
                   lambda h: v_ref[:, h * V_DIM:(h + 1) * V_DIM], True)

    accumulate(lambda h: (kp_ref[pl.ds(2 * h, tp, stride=hm), :].astype(BF16),
                          kp_ref[pl.ds(2 * h + 1, tp, stride=hm), :].astype(BF16)),
               lambda h: vp_ref[pl.ds(h, tp, stride=N_HEADS), :].astype(BF16), False)

    @pl.when(pt == pl.num_programs(1) - 1)
    def _():
        lam, lam_init = _lam_value(lam_ref, layer_idx)
        g = g_ref[...]
        for h in range(N_HEADS):
            r1 = pl.ds(2 * n_q * h, n_q)
            r2 = pl.ds(2 * n_q * h + n_q, n_q)
            out = acc_ref[r1, :] / l_ref[r1, :] - acc_ref[r2, :] * (lam / l_ref[r2, :])
            ms = jnp.mean(out * out, axis=-1, keepdims=True)
            o_ref[:, h * V_DIM:(h + 1) * V_DIM] = (out * lax.rsqrt(ms + RMS_EPS) * g * (1.0 - lam_init)).astype(BF16)


def _attn_sample(q, cache_k, cache_v, kb, vb, attn_lam, subln_g, layer_idx):
    nb, length, _ = q.shape
    past = cache_k.shape[1]
    assert past % CHUNK == 0 and length <= CHUNK
    tp = PAST_TILE if past % PAST_TILE == 0 else past
    new = pl.BlockSpec((None, length, D_MODEL), lambda b, p: (b, 0, 0))
    hm = 2 * N_HEADS
    return pl.pallas_call(
        functools.partial(_attn_sample_body, layer_idx=layer_idx, tp=tp),
        grid=(nb, past // tp),
        in_specs=[new,
                  pl.BlockSpec((None, tp * hm, HEAD_DIM), lambda b, p: (b, p, 0)),
                  pl.BlockSpec((None, tp * N_HEADS, V_DIM), lambda b, p: (b, p, 0)),
                  new, new, _const_spec((4, HEAD_DIM)), _const_spec((1, V_DIM))],
        out_specs=new,
        out_shape=jax.ShapeDtypeStruct((nb, length, D_MODEL), BF16),
        scratch_shapes=[pltpu.VMEM((hm * length, 1), F32), pltpu.VMEM((hm * length, 1), F32),
                        pltpu.VMEM((hm * length, V_DIM), F32)],
        compiler_params=_params(2),
        name="attn_sample",
    )(q, cache_k.reshape(nb, past * hm, HEAD_DIM), cache_v.reshape(nb, past * N_HEADS, V_DIM), kb, vb, attn_lam, subln_g)


def _proj_ln_body(x_ref, o_ref_in, w_ref, g_ref, b_ref, out_ref):
    y = _dot(o_ref_in[...], w_ref[...])
    out_ref[...] = _layer_norm(ALPHA * x_ref[...] + y, g_ref[...], b_ref[...])


def _proj_ln(x, o, w_o, g, b):
    n = x.shape[0]
    tm = _row_tile(n)
    row = pl.BlockSpec((tm, D_MODEL), lambda i: (i, 0))
    vec = _const_spec((1, D_MODEL))
    return pl.pallas_call(
        _proj_ln_body,
        grid=(n // tm,),
        in_specs=[row, row, _const_spec((D_MODEL, D_MODEL)), vec, vec],
        out_specs=row,
        out_shape=jax.ShapeDtypeStruct((n, D_MODEL), F32),
        compiler_params=_params(1),
        name="proj_ln",
    )(x, o, w_o, g, b)


def _run_trunk(x, pos, h0_re, h0_im, cache_k, cache_v, w):
    nb, length, _ = x.shape
    n = nb * length
    assert length % S5_T == 0
    ln_g, ln_b = w['ln_g'], w['ln_b']

    def vec(v):
        return v.reshape(1, D_MODEL).astype(F32)

    def ffn(t, l, j, ln_idx):
        return _ffn_ln(t, w['ffn_w_in'][l, j], w['ffn_w_out'][l, j], vec(ln_g[l, ln_idx]), vec(ln_b[l, ln_idx]))

    t = x.reshape(n, D_MODEL)
    new_re, new_im = [], []
    k = v = None
    for l in range(DEPTH):
        if l == N_A:
            k, v, kb, vb = _rope_call(
                _kv_rope_body, t, w['attn_w_kv'], pos, length,
                ((2 * N_HEADS, (HEAD_DIM,), F32), (N_HEADS, (V_DIM,), F32), (1, (D_MODEL,), BF16), (1, (D_MODEL,), BF16)),
                "kv_rope")
        t = ffn(t, l, 0, 0)
        if l < N_A:
            ys, hr, hi = _s5_core(t, nb, h0_re[l], h0_im[l], w['ssm'][l])
            new_re.append(hr)
            new_im.append(hi)
            t = _glu_ln(t, ys, vec(w['ssm_d'][l]), w['ssm_w_glu'][l],
                        vec(ln_g[l, 1]), vec(ln_b[l, 1]))
        else:
            j = l - N_A
            (q,) = _rope_call(_q_rope_body, t, w['attn_w_q'][j], pos, length, ((1, (D_MODEL,), BF16),), "q_rope")
            q = q.reshape(nb, length, D_MODEL)
            kb3 = kb.reshape(nb, length, D_MODEL)
            vb3 = vb.reshape(nb, length, D_MODEL)
            lam_p = w['attn_lam'][j].astype(F32)
            sub_g = w['attn_subln_g'][j].reshape(1, V_DIM).astype(F32)
            if cache_k is None:
                o = _attn_prompt(q, kb3, vb3, lam_p, sub_g, l)
            else:
                o = _attn_sample(q, cache_k, cache_v, kb3, vb3, lam_p, sub_g, l)
            t = _proj_ln(t, o.reshape(n, D_MODEL), w['attn_w_o'][j], vec(ln_g[l, 1]), vec(ln_b[l, 1]))
        t = ffn(t, l, 1, 2)
    y = t.reshape(nb, length, D_MODEL)
    p_k = k.reshape(nb, length, N_HEADS, 2, HEAD_DIM)
    p_v = v.reshape(nb, length, N_HEADS, V_DIM)
    return y, jnp.stack(new_re), jnp.stack(new_im), p_k, p_v


def kernel(x_prompt, x_sample, cache_k, cache_v, state_ssm_re, state_ssm_im,
           ln_g, ln_b, ffn_w_in, ffn_w_out,
           ssm_a_re, ssm_a_im, ssm_log_dt, ssm_b_re, ssm_b_im, ssm_c_re, ssm_c_im, ssm_d, ssm_w_glu,
           attn_w_q, attn_w_kv, attn_lam, attn_subln_g, attn_w_o):
    w = {'ln_g': ln_g, 'ln_b': ln_b,
         'ffn_w_in': ffn_w_in.astype(BF16), 'ffn_w_out': ffn_w_out.astype(BF16),
         'ssm': [_s5_weights(ssm_a_re[l], ssm_a_im[l], ssm_log_dt[l], ssm_b_re[l], ssm_b_im[l],
                             ssm_c_re[l], ssm_c_im[l]) for l in range(N_A)],
         'ssm_d': ssm_d, 'ssm_w_glu': ssm_w_glu.astype(BF16),
         'attn_w_q': attn_w_q.astype(BF16), 'attn_w_kv': attn_w_kv.astype(BF16), 'attn_lam': attn_lam,
         'attn_subln_g': attn_subln_g, 'attn_w_o': attn_w_o.astype(BF16)}
    b_p, l_p, _ = x_prompt.shape
    pos_p = jnp.arange(l_p, dtype=jnp.int32)
    h0 = jnp.zeros((N_A, b_p, N_GROUPS, P_STATE), F32)
    y_p, p_re, p_im, p_k, p_v = _run_trunk(x_prompt, pos_p, h0, h0, None, None, w)
    past = cache_k.shape[1]
    pos_s = past + jnp.arange(x_sample.shape[1], dtype=jnp.int32)
    y_s, s_re, s_im, s_k, s_v = _run_trunk(x_sample, pos_s, state_ssm_re, state_ssm_im, cache_k, cache_v, w)
    return (y_p, y_s, p_re, p_im, p_k, p_v, s_re, s_im, s_k, s_v)
```

```python
import functools
import math

import jax
import jax.numpy as jnp
from jax import lax
from jax.experimental import pallas as pl
from jax.experimental.pallas import tpu as pltpu

F32 = jnp.float32
BF16 = jnp.bfloat16

D_MODEL = 1024
DEPTH = 2
CHUNK = 64
N_A = DEPTH // 2
GROUP = 16
N_GROUPS = D_MODEL // GROUP
P_STATE = 64
HEAD_DIM = 64
V_DIM = 2 * HEAD_DIM
N_HEADS = D_MODEL // V_DIM
D_FF = ((8 * D_MODEL // 3 + 127) // 128) * 128
ROPE_THETA = 10000.0
ALPHA = (2 * DEPTH) ** 0.25
LN_EPS = 1e-5
RMS_EPS = 1e-5
NEG_INF = -1e30
LOG2_E = math.log2(math.e)

LANES = 128
ROW_TILE = 512
FF_TILE = 256
S5_T = 16
SLAB_GROUPS = LANES // GROUP
N_SLABS = N_GROUPS // SLAB_GROUPS
S5_ROWS = 512
Q_TILE = 256
PAST_TILE = 1024
VMEM_LIMIT = 56 * 1024 * 1024


def _params(n_axes):
    return pltpu.CompilerParams(dimension_semantics=("arbitrary",) * n_axes, vmem_limit_bytes=VMEM_LIMIT)


def _row_tile(n):
    return ROW_TILE if n % ROW_TILE == 0 else n


def _const_spec(shape):
    return pl.BlockSpec(shape, lambda *_: (0,) * len(shape), pipeline_mode=pl.Buffered(1))


def _layer_norm(v, g, b):
    mu = jnp.mean(v, axis=-1, keepdims=True)
    c = v - mu
    var = jnp.mean(c * c, axis=-1, keepdims=True)
    return c * lax.rsqrt(var + LN_EPS) * g + b


def _sigmoid(v):
    return 0.5 * jnp.tanh(0.5 * v) + 0.5


def _gelu_tanh(v):
    return 0.5 * v * (1.0 + jnp.tanh(math.sqrt(2.0 / math.pi) * (v + 0.044715 * (v * v * v))))


def _dot(a, b):
    return jnp.dot(a, b, preferred_element_type=F32)


def _dot_nt(a, b):
    return lax.dot_general(a, b, (((1,), (1,)), ((), ())), preferred_element_type=F32)


def _ffn_ln_body(x_ref, win_ref, wout_ref, g_ref, b_ref, o_ref):
    x = x_ref[...]
    xb = x.astype(BF16)
    acc = None
    for c in range(D_FF // FF_TILE):
        a = _dot(xb, win_ref[:, pl.ds(c * FF_TILE, FF_TILE)])
        gate = _dot(xb, win_ref[:, pl.ds(D_FF + c * FF_TILE, FF_TILE)])
        h = (a * _sigmoid(a) * gate).astype(BF16)
        part = _dot(h, wout_ref[pl.ds(c * FF_TILE, FF_TILE), :])
        acc = part if acc is None else acc + part
    o_ref[...] = _layer_norm(ALPHA * x + 0.5 * acc, g_ref[...], b_ref[...])


def _ffn_ln(x, w_in, w_out, g, b):
    n = x.shape[0]
    tm = _row_tile(n)
    row = pl.BlockSpec((tm, D_MODEL), lambda i: (i, 0))
    return pl.pallas_call(
        _ffn_ln_body,
        grid=(n // tm,),
        in_specs=[row, _const_spec((D_MODEL, 2 * D_FF)), _const_spec((D_FF, D_MODEL)),
                  _const_spec((1, D_MODEL)), _const_spec((1, D_MODEL))],
        out_specs=row,
        out_shape=jax.ShapeDtypeStruct((n, D_MODEL), F32),
        compiler_params=_params(1),
        name="ffn_ln",
    )(x, w_in, w_out, g, b)


def _s5_weights(a_re, a_im, log_dt, b_re, b_im, c_re, c_im):
    hi = lax.Precision.HIGHEST
    lam_re = jnp.minimum(a_re.astype(F32), -1e-4)
    lam_im = a_im.astype(F32)
    dt = jnp.exp(log_dt.astype(F32))[:, None]
    mag = jnp.exp(lam_re * dt)
    lb_re = mag * jnp.cos(lam_im * dt)
    lb_im = mag * jnp.sin(lam_im * dt)
    den = lam_re * lam_re + lam_im * lam_im
    n_re = lb_re - 1.0
    k_re = (n_re * lam_re + lb_im * lam_im) / den
    k_im = (lb_im * lam_re - n_re * lam_im) / den
    b_re = b_re.astype(F32)
    b_im = b_im.astype(F32)
    bb_re = k_re[..., None] * b_re - k_im[..., None] * b_im
    bb_im = k_re[..., None] * b_im + k_im[..., None] * b_re
    c_re = c_re.astype(F32)
    c_im = c_im.astype(F32)

    def power_step(carry, _):
        pr, pi = carry
        return (pr * lb_re - pi * lb_im, pr * lb_im + pi * lb_re), (pr, pi)

    _, (pw_re, pw_im) = lax.scan(power_step, (jnp.ones_like(lb_re), jnp.zeros_like(lb_re)), None,
                                 length=S5_T + 1)

    cl_re = c_re[None] * pw_re[:, :, None, :] - c_im[None] * pw_im[:, :, None, :]
    cl_im = c_re[None] * pw_im[:, :, None, :] + c_im[None] * pw_re[:, :, None, :]

    taps = (jnp.einsum('ngdp,gpc->ngcd', cl_re[:S5_T], bb_re, precision=hi)
            - jnp.einsum('ngdp,gpc->ngcd', cl_im[:S5_T], bb_im, precision=hi))
    state_w = SLAB_GROUPS * 2 * P_STATE

    lane_of = jnp.arange(LANES)[None, :] % GROUP
    repeat_lanes = (lane_of == jnp.arange(GROUP)[:, None]).astype(F32)
    w_toep = jnp.einsum('ngcd,dl->ngcl', taps, repeat_lanes, precision=hi)
    w_toep = w_toep.reshape(S5_T, N_SLABS, LANES, LANES).transpose(1, 0, 2, 3).astype(BF16)

    rev_re = pw_re[S5_T - 1::-1][:S5_T]
    rev_im = pw_im[S5_T - 1::-1][:S5_T]
    in_re = rev_re[..., None] * bb_re[None] - rev_im[..., None] * bb_im[None]
    in_im = rev_re[..., None] * bb_im[None] + rev_im[..., None] * bb_re[None]
    w_in = jnp.concatenate([in_re, in_im], axis=2)
    w_in = jnp.einsum('ck,sgpk->csgp', jnp.eye(GROUP, dtype=F32), w_in, precision=hi).transpose(1, 2, 0, 3)
    w_in = w_in.reshape(S5_T, N_SLABS, LANES, 2 * P_STATE).transpose(1, 0, 2, 3).astype(BF16)

    w_out = jnp.concatenate([cl_re[1:], -cl_im[1:]], axis=3)
    w_out = w_out.reshape(S5_T, N_SLABS, SLAB_GROUPS, GROUP, 2 * P_STATE).transpose(1, 0, 4, 2, 3)
    w_out = w_out.reshape(N_SLABS, S5_T, 2 * P_STATE, LANES).astype(BF16)

    lt_re, lt_im = pw_re[S5_T], pw_im[S5_T]
    coef = jnp.stack([jnp.concatenate([lt_re, lt_re], -1),
                      jnp.concatenate([-lt_im, lt_im], -1),
                      jnp.concatenate([lt_im, -lt_im], -1)], axis=0)
    coef = coef.reshape(3, N_SLABS, state_w).transpose(1, 0, 2)
    coef = jnp.concatenate([coef, jnp.zeros((N_SLABS, 5, state_w), F32)], axis=1)
    return w_toep, w_in, w_out, coef


def _s5_expand_weights(toep_ref, cin_ref, cout_ref, wt_ref, win_ref, wout_ref):
    group_of_row = lax.broadcasted_iota(jnp.int32, (LANES, LANES), 0) // GROUP
    group_of_col = lax.broadcasted_iota(jnp.int32, (LANES, LANES), 1) // GROUP
    zero = jnp.zeros((LANES, LANES), BF16)
    taps = [jnp.where(group_of_row == group_of_col, toep_ref[n], zero) for n in range(S5_T)]
    for s in range(S5_T):
        rows = pl.ds(s * LANES, LANES)
        for t in range(S5_T):
            wt_ref[rows, pl.ds(t * LANES, LANES)] = taps[t - s] if t >= s else zero
        tile_in = cin_ref[s]
        tile_out = cout_ref[s]
        for j in range(SLAB_GROUPS):
            win_ref[rows, pl.ds(j * LANES, LANES)] = jnp.where(group_of_row == j, tile_in, zero)
            wout_ref[pl.ds(j * LANES, LANES), rows] = jnp.where(group_of_col == j, tile_out, zero)


def _s5_body(x_ref, toep_ref, cin_ref, cout_ref, coef_ref, h0_ref, y_ref, hfin_ref,
             wt_ref, win_ref, wout_ref, s_ref, hst_ref, *, rows, n_chunks, bt):
    @pl.when(pl.program_id(1) == 0)
    def _():
        _s5_expand_weights(toep_ref, cin_ref, cout_ref, wt_ref, win_ref, wout_ref)

    xcat = jnp.concatenate([x_ref[pl.ds(s, rows, stride=S5_T), :].astype(BF16) for s in range(S5_T)], axis=1)
    s_all = _dot(xcat, win_ref[...])
    for j in range(SLAB_GROUPS):
        sj = s_all[:, j * LANES:(j + 1) * LANES]
        s_ref[j] = sj
        s_ref[SLAB_GROUPS + j] = pltpu.roll(sj, P_STATE, 1)
    coef = coef_ref[...]
    h0 = h0_ref[...]

    def lanes(v, j):
        return v[:, j * LANES:(j + 1) * LANES]

    def step(k, carry):
        hs, hts = carry
        idx = pl.ds(k, bt, stride=n_chunks)
        new_h, new_ht = [], []
        for j in range(SLAB_GROUPS):
            hst_ref[j, idx, :] = hs[j]
            c_self, c_swap, c_swap_t = lanes(coef[0:1], j), lanes(coef[1:2], j), lanes(coef[2:3], j)
            new_h.append(c_self * hs[j] + c_swap * hts[j] + s_ref[j, idx, :])
            new_ht.append(c_self * hts[j] + c_swap_t * hs[j] + s_ref[SLAB_GROUPS + j, idx, :])
        return tuple(new_h), tuple(new_ht)

    init = (tuple(lanes(h0[0], j) for j in range(SLAB_GROUPS)), tuple(lanes(h0[1], j) for j in range(SLAB_GROUPS)))
    hs, _ = lax.fori_loop(0, n_chunks, step, init)
    hfin_ref[...] = jnp.concatenate(hs, axis=1)
    hst = jnp.concatenate([hst_ref[j] for j in range(SLAB_GROUPS)], axis=1).astype(BF16)
    for t in range(0, S5_T, 2):
        cols = pl.ds(t * LANES, 2 * LANES)
        k_used = (t + 2) * LANES
        y = _dot(xcat[:, :k_used], wt_ref[pl.ds(0, k_used), cols]) + _dot(hst, wout_ref[:, cols])
        y_ref[pl.ds(t, rows, stride=S5_T), :] = y[:, :LANES]
        y_ref[pl.ds(t + 1, rows, stride=S5_T), :] = y[:, LANES:]


def _s5_batch_tile(nb, n_chunks):
    bt = nb
    while bt * n_chunks > S5_ROWS and bt % 2 == 0:
        bt //= 2
    return bt


def _s5_core(u, nb, h0_re, h0_im, weights):
    w_toep, w_in, w_out, coef = weights
    n = u.shape[0]
    length = n // nb
    n_chunks = length // S5_T
    bt = _s5_batch_tile(nb, n_chunks)
    n_bt = nb // bt
    rows = bt * n_chunks
    wide = S5_T * LANES
    sw = SLAB_GROUPS * 2 * P_STATE

    def slab_state(re, im):
        v = jnp.concatenate([re, im], axis=-1).astype(F32)
        return v.reshape(n_bt, bt, N_SLABS, sw).transpose(2, 0, 1, 3)

    h0 = jnp.stack([slab_state(h0_re, h0_im), slab_state(h0_im, h0_re)], axis=2)

    def per_slab(shape):
        return pl.BlockSpec((None,) + shape, lambda s, i: (s,) + (0,) * len(shape), pipeline_mode=pl.Buffered(1))

    tokens = pl.BlockSpec((bt * length, LANES), lambda s, i: (i, s))
    y, hfin = pl.pallas_call(
        functools.partial(_s5_body, rows=rows, n_chunks=n_chunks, bt=bt),
        grid=(N_SLABS, n_bt),
        in_specs=[tokens, per_slab((S5_T, LANES, LANES)), per_slab((S5_T, LANES, 2 * P_STATE)),
                  per_slab((S5_T, 2 * P_STATE, LANES)), per_slab((8, sw)),
                  pl.BlockSpec((None, None, 2, bt, sw), lambda s, i: (s, i, 0, 0, 0))],
        out_specs=[tokens, pl.BlockSpec((None, None, bt, sw), lambda s, i: (s, i, 0, 0))],
        out_shape=[jax.ShapeDtypeStruct((n, D_MODEL), F32), jax.ShapeDtypeStruct((N_SLABS, n_bt, bt, sw), F32)],
        scratch_shapes=[pltpu.VMEM((wide, wide), BF16), pltpu.VMEM((wide, sw), BF16), pltpu.VMEM((sw, wide), BF16),
                        pltpu.VMEM((2 * SLAB_GROUPS, rows, LANES), F32), pltpu.VMEM((SLAB_GROUPS, rows, LANES), F32)],
        compiler_params=_params(2),
        name="s5_core",
    )(u, w_toep, w_in, w_out, coef, h0)
    hfin = hfin.transpose(1, 2, 0, 3).reshape(nb, N_GROUPS, 2 * P_STATE)
    return y, hfin[..., :P_STATE], hfin[..., P_STATE:]


def _glu_ln_body(x_ref, ys_ref, d_ref, w_ref, g_ref, b_ref, o_ref):
    x = x_ref[...]
    z = _gelu_tanh(ys_ref[...] + d_ref[...] * x).astype(BF16)
    out = _dot(z, w_ref[:, pl.ds(0, D_MODEL)])
    gate = _dot(z, w_ref[:, pl.ds(D_MODEL, D_MODEL)])
    o_ref[...] = _layer_norm(ALPHA * x + out * _sigmoid(gate), g_ref[...], b_ref[...])


def _glu_ln(x, ys, d_skip, w_glu, g, b):
    n = x.shape[0]
    tm = _row_tile(n)
    row = pl.BlockSpec((tm, D_MODEL), lambda i: (i, 0))
    vec = _const_spec((1, D_MODEL))
    return pl.pallas_call(
        _glu_ln_body,
        grid=(n // tm,),
        in_specs=[row, row, vec, _const_spec((D_MODEL, 2 * D_MODEL)), vec, vec],
        out_specs=row,
        out_shape=jax.ShapeDtypeStruct((n, D_MODEL), F32),
        compiler_params=_params(1),
        name="glu_ln",
    )(x, ys, d_skip, w_glu, g, b)


def _rope_tables(pos, rows):
    half = HEAD_DIM // 2
    inv = ROPE_THETA ** (-jnp.arange(half, dtype=F32) / half)
    ang = pos.astype(F32)[:, None] * inv[None, :]
    cos = jnp.cos(ang)
    sin = jnp.sin(ang)
    cos_t = jnp.concatenate([cos, cos, cos, cos], axis=-1)
    sin_t = jnp.concatenate([-sin, sin, -sin, sin], axis=-1)
    reps = rows // pos.shape[0]
    return jnp.tile(cos_t, (reps, 1)), jnp.tile(sin_t, (reps, 1))


def _rope_slab(v, cos, sin, first_half):
    partner = jnp.where(first_half, pltpu.roll(v, LANES - HEAD_DIM // 2, 1), pltpu.roll(v, HEAD_DIM // 2, 1))
    return v * cos + partner * sin


def _first_half_mask(rows):
    lane = lax.broadcasted_iota(jnp.int32, (rows, LANES), 1)
    return (lane % HEAD_DIM) < (HEAD_DIM // 2)


def _kv_rope_body(x_ref, wk_ref, wv_ref, cos_ref, sin_ref, kt_ref, ktb_ref, v_ref, vb_ref):
    xb = x_ref[...].astype(BF16)
    tm = xb.shape[0]
    half = HEAD_DIM // 2
    cos = cos_ref[...]
    sin = sin_ref[...]
    kt = _dot_nt(wk_ref[...], xb)
    for h in range(N_HEADS):
        for mp in range(2):
            lo = (2 * h + mp) * HEAD_DIM
            x1 = kt[lo:lo + half]
            x2 = kt[lo + half:lo + HEAD_DIM]
            r1 = x1 * cos - x2 * sin
            r2 = x1 * sin + x2 * cos
            kt_ref[h, mp, pl.ds(0, half), :] = r1
            kt_ref[h, mp, pl.ds(half, half), :] = r2
            ktb_ref[h, pl.ds(mp * HEAD_DIM, half), :] = r1.astype(BF16)
            ktb_ref[h, pl.ds(mp * HEAD_DIM + half, half), :] = r2.astype(BF16)
    for h in range(N_HEADS):
        cols = pl.ds(h * V_DIM, V_DIM)
        v = _dot(xb, wv_ref[:, cols])
        v_ref[pl.ds(h, tm, stride=N_HEADS), :] = v
        vb_ref[:, cols] = v.astype(BF16)


def _kv_rope(x, nb, w_kv, pos):
    n = x.shape[0]
    length = n // nb
    tm = min(ROW_TILE, length)
    tiles = length // tm
    half = HEAD_DIM // 2
    inv = ROPE_THETA ** (-jnp.arange(half, dtype=F32) / half)
    ang = inv[:, None] * pos.astype(F32)[None, :]
    wk_t = w_kv[:, :D_MODEL].T
    w_v = w_kv[:, D_MODEL:]
    table = pl.BlockSpec((half, tm), lambda i: (0, i % tiles))
    return pl.pallas_call(
        _kv_rope_body,
        grid=(n // tm,),
        in_specs=[pl.BlockSpec((tm, D_MODEL), lambda i: (i, 0)), _const_spec((D_MODEL, D_MODEL)),
                  _const_spec((D_MODEL, D_MODEL)), table, table],
        out_specs=[pl.BlockSpec((None, N_HEADS, 2, HEAD_DIM, tm), lambda i: (i // tiles, 0, 0, 0, i % tiles)),
                   pl.BlockSpec((None, N_HEADS, V_DIM, tm), lambda i: (i // tiles, 0, 0, i % tiles)),
                   pl.BlockSpec((tm * N_HEADS, V_DIM), lambda i: (i, 0)),
                   pl.BlockSpec((tm, D_MODEL), lambda i: (i, 0))],
        out_shape=[jax.ShapeDtypeStruct((nb, N_HEADS, 2, HEAD_DIM, length), F32),
                   jax.ShapeDtypeStruct((nb, N_HEADS, V_DIM, length), BF16),
                   jax.ShapeDtypeStruct((n * N_HEADS, V_DIM), F32),
                   jax.ShapeDtypeStruct((n, D_MODEL), BF16)],
        compiler_params=_params(1),
        name="kv_rope",
    )(x, wk_t, w_v, jnp.cos(ang), jnp.sin(ang))


def _q_rope_body(x_ref, w_ref, cos_ref, sin_ref, q_ref):
    xb = x_ref[...].astype(BF16)
    cos = cos_ref[...]
    sin = sin_ref[...]
    first_half = _first_half_mask(xb.shape[0])
    scale = HEAD_DIM ** -0.5 * LOG2_E
    for j in range(D_MODEL // LANES):
        cols = pl.ds(j * LANES, LANES)
        q = _rope_slab(_dot(xb, w_ref[:, cols]), cos, sin, first_half)
        q_ref[:, cols] = (q * scale).astype(BF16)


def _rope_call(body, x, w, pos, length, outs, name):
    n = x.shape[0]
    tm = _row_tile(n)
    table_rows = max(length, tm)
    assert table_rows % length == 0 and table_rows % tm == 0
    cos_t, sin_t = _rope_tables(pos, table_rows)
    n_table_tiles = table_rows // tm

    def rows(per_token, trailing):
        zeros = (0,) * len(trailing)
        return pl.BlockSpec((tm * per_token,) + trailing, lambda i: (i,) + zeros)

    table = pl.BlockSpec((tm, LANES), lambda i: (i % n_table_tiles, 0))
    return pl.pallas_call(
        body,
        grid=(n // tm,),
        in_specs=[rows(1, (D_MODEL,)), _const_spec(w.shape), table, table],
        out_specs=[rows(per_token, trailing) for per_token, trailing, _ in outs],
        out_shape=[jax.ShapeDtypeStruct((n * per_token,) + trailing, dt) for per_token, trailing, dt in outs],
        compiler_params=_params(1),
        name=name,
    )(x, w, cos_t, sin_t)


def _lam_value(lam_ref, layer_idx):
    lv = lam_ref[...]
    s01 = jnp.sum(lv[0:1, :] * lv[1:2, :], axis=-1, keepdims=True)
    s23 = jnp.sum(lv[2:3, :] * lv[3:4, :], axis=-1, keepdims=True)
    lam_init = 0.8 - 0.6 * math.exp(-0.3 * layer_idx)
    return jnp.exp(s01) - jnp.exp(s23) + lam_init, lam_init


def _diff_attend(q, segments, lam, lam_init, subln_g):
    lane = lax.broadcasted_iota(jnp.int32, q.shape, 1)
    zero = jnp.zeros_like(q)
    q1 = jnp.where(lane < HEAD_DIM, q, zero)
    q2 = jnp.where(lane >= HEAD_DIM, q, zero)
    maps = []
    for qm in (q1, q2):
        scores = []
        for kt, _, mask in segments:
            s = _dot(qm, kt)
            scores.append(s if mask is None else jnp.where(mask, s, NEG_INF))
        m = functools.reduce(jnp.maximum, [jnp.max(s, axis=-1, keepdims=True) for s in scores])
        es = [jnp.exp2(s - m) for s in scores]
        denom = functools.reduce(jnp.add, [jnp.sum(e, axis=-1, keepdims=True) for e in es])
        acc = functools.reduce(jnp.add, [_dot(e.astype(BF16), v) for e, (_, v, _) in zip(es, segments)])
        maps.append(acc / denom)
    out = maps[0] - lam * maps[1]
    ms = jnp.mean(out * out, axis=-1, keepdims=True)
    return out * lax.rsqrt(ms + RMS_EPS) * subln_g * (1.0 - lam_init)


def _attn_prompt_body(q_ref, kt_ref, v_ref, lam_ref, g_ref, o_ref, *, length, tq, layer_idx):
    lam, lam_init = _lam_value(lam_ref, layer_idx)
    g = g_ref[...]
    r = lax.broadcasted_iota(jnp.int32, (tq, tq), 0) // CHUNK
    c = lax.broadcasted_iota(jnp.int32, (tq, tq), 1) // CHUNK
    diag_mask = c <= r
    for i in range(length // tq):
        rows = pl.ds(i * tq, tq)
        segments = []
        if i > 0:
            segments.append((kt_ref[:, pl.ds(0, i * tq)], v_ref[pl.ds(0, i * tq), :], None))
        segments.append((kt_ref[:, rows], v_ref[rows, :], diag_mask))
        o_ref[rows, :] = _diff_attend(q_ref[rows, :], segments, lam, lam_init, g).astype(BF16)


def _attn_prompt(q, ktb, vb, attn_lam, subln_g, layer_idx):
    nb, length, _ = q.shape
    tq = min(Q_TILE, length)
    head = pl.BlockSpec((None, length, V_DIM), lambda b, h: (b, 0, h))
    head_t = pl.BlockSpec((None, None, V_DIM, length), lambda b, h: (b, h, 0, 0))
    return pl.pallas_call(
        functools.partial(_attn_prompt_body, length=length, tq=tq, layer_idx=layer_idx),
        grid=(nb, N_HEADS),
        in_specs=[head, head_t, head, _const_spec((4, HEAD_DIM)), _const_spec((1, V_DIM))],
        out_specs=head,
        out_shape=jax.ShapeDtypeStruct((nb, length, D_MODEL), BF16),
        compiler_params=_params(2),
        name="attn_prompt",
    )(q, ktb, vb, attn_lam, subln_g)


def _attn_sample_body(q_ref, kp_ref, vp_ref, kt_ref, v_ref, lam_ref, g_ref, o_ref, m_ref, l_ref, acc_ref, *, layer_idx, tp):
    pt = pl.program_id(1)
    n_q = q_ref.shape[0]
    zero = jnp.zeros((n_q, HEAD_DIM), BF16)

    def head_scores(h, k1, k2):
        q1 = q_ref[:, h * V_DIM:h * V_DIM + HEAD_DIM]
        q2 = q_ref[:, h * V_DIM + HEAD_DIM:(h + 1) * V_DIM]
        return _dot(jnp.concatenate([q1, zero], axis=0), k1) + _dot(jnp.concatenate([zero, q2], axis=0), k2)

    def accumulate(keys_of, values_of, first):
        s = jnp.concatenate([head_scores(h, *keys_of(h)) for h in range(N_HEADS)], axis=0)
        m_tile = jnp.max(s, axis=-1, keepdims=True)
        if first:
            m_new = m_tile
        else:
            m_old = m_ref[...]
            m_new = jnp.maximum(m_old, m_tile)
            alpha = jnp.exp2(m_old - m_new)
        e = jnp.exp2(s - m_new)
        l_tile = jnp.sum(e, axis=-1, keepdims=True)
        eb = e.astype(BF16)
        pv = jnp.concatenate([_dot(eb[2 * n_q * h:2 * n_q * (h + 1)], values_of(h)) for h in range(N_HEADS)], axis=0)
        m_ref[...] = m_new
        l_ref[...] = l_tile if first else alpha * l_ref[...] + l_tile
        acc_ref[...] = pv if first else alpha * acc_ref[...] + pv

    @pl.when(pt == 0)
    def _():
        accumulate(lambda h: (kt_ref[h, pl.ds(0, HEAD_DIM), :], kt_ref[h, pl.ds(HEAD_DIM, HEAD_DIM), :]),
                   lambda h: v_ref[:, h * V_DIM:(h + 1) * V_DIM], True)

    accumulate(lambda h: (kp_ref[h, 0].astype(BF16), kp_ref[h, 1].astype(BF16)),
               lambda h: vp_ref[pl.ds(h, tp, stride=N_HEADS), :].astype(BF16), False)

    @pl.when(pt == pl.num_programs(1) - 1)
    def _():
        lam, lam_init = _lam_value(lam_ref, layer_idx)
        g = g_ref[...]
        for h in range(N_HEADS):
            r1 = pl.ds(2 * n_q * h, n_q)
            r2 = pl.ds(2 * n_q * h + n_q, n_q)
            out = acc_ref[r1, :] / l_ref[r1, :] - acc_ref[r2, :] * (lam / l_ref[r2, :])
            ms = jnp.mean(out * out, axis=-1, keepdims=True)
            o_ref[:, h * V_DIM:(h + 1) * V_DIM] = (out * lax.rsqrt(ms + RMS_EPS) * g * (1.0 - lam_init)).astype(BF16)


def _attn_sample(q, cache_k, cache_v, ktb, vb, attn_lam, subln_g, layer_idx):
    nb, length, _ = q.shape
    past = cache_k.shape[1]
    assert past % CHUNK == 0 and length <= CHUNK
    tp = PAST_TILE if past % PAST_TILE == 0 else past
    new = pl.BlockSpec((None, length, D_MODEL), lambda b, p: (b, 0, 0))
    hm = 2 * N_HEADS
    return pl.pallas_call(
        functools.partial(_attn_sample_body, layer_idx=layer_idx, tp=tp),
        grid=(nb, past // tp),
        in_specs=[new,
                  pl.BlockSpec((None, N_HEADS, 2, HEAD_DIM, tp), lambda b, p: (b, 0, 0, 0, p)),
                  pl.BlockSpec((None, tp * N_HEADS, V_DIM), lambda b, p: (b, p, 0)),
                  pl.BlockSpec((None, N_HEADS, V_DIM, length), lambda b, p: (b, 0, 0, 0)),
                  new, _const_spec((4, HEAD_DIM)), _const_spec((1, V_DIM))],
        out_specs=new,
        out_shape=jax.ShapeDtypeStruct((nb, length, D_MODEL), BF16),
        scratch_shapes=[pltpu.VMEM((hm * length, 1), F32), pltpu.VMEM((hm * length, 1), F32),
                        pltpu.VMEM((hm * length, V_DIM), F32)],
        compiler_params=_params(2),
        name="attn_sample",
    )(q, cache_k.transpose(0, 2, 3, 4, 1), cache_v.reshape(nb, past * N_HEADS, V_DIM), ktb, vb, attn_lam, subln_g)


def _proj_ln_body(x_ref, o_ref_in, w_ref, g_ref, b_ref, out_ref):
    y = _dot(o_ref_in[...], w_ref[...])
    out_ref[...] = _layer_norm(ALPHA * x_ref[...] + y, g_ref[...], b_ref[...])


def _proj_ln(x, o, w_o, g, b):
    n = x.shape[0]
    tm = _row_tile(n)
    row = pl.BlockSpec((tm, D_MODEL), lambda i: (i, 0))
    vec = _const_spec((1, D_MODEL))
    return pl.pallas_call(
        _proj_ln_body,
        grid=(n // tm,),
        in_specs=[row, row, _const_spec((D_MODEL, D_MODEL)), vec, vec],
        out_specs=row,
        out_shape=jax.ShapeDtypeStruct((n, D_MODEL), F32),
        compiler_params=_params(1),
        name="proj_ln",
    )(x, o, w_o, g, b)


def _run_trunk(x, pos, h0_re, h0_im, cache_k, cache_v, w):
    nb, length, _ = x.shape
    n = nb * length
    assert length % S5_T == 0
    ln_g, ln_b = w['ln_g'], w['ln_b']

    def vec(v):
        return v.reshape(1, D_MODEL).astype(F32)

    def ffn(t, l, j, ln_idx):
        return _ffn_ln(t, w['ffn_w_in'][l, j], w['ffn_w_out'][l, j], vec(ln_g[l, ln_idx]), vec(ln_b[l, ln_idx]))

    t = x.reshape(n, D_MODEL)
    new_re, new_im = [], []
    kt = v = None
    for l in range(DEPTH):
        if l == N_A:
            kt, ktb, v, vb = _kv_rope(t, nb, w['attn_w_kv'], pos)
        t = ffn(t, l, 0, 0)
        if l < N_A:
            ys, hr, hi = _s5_core(t, nb, h0_re[l], h0_im[l], w['ssm'][l])
            new_re.append(hr)
            new_im.append(hi)
            t = _glu_ln(t, ys, vec(w['ssm_d'][l]), w['ssm_w_glu'][l],
                        vec(ln_g[l, 1]), vec(ln_b[l, 1]))
        else:
            j = l - N_A
            (q,) = _rope_call(_q_rope_body, t, w['attn_w_q'][j], pos, length, ((1, (D_MODEL,), BF16),), "q_rope")
            q = q.reshape(nb, length, D_MODEL)
            vb3 = vb.reshape(nb, length, D_MODEL)
            lam_p = w['attn_lam'][j].astype(F32)
            sub_g = w['attn_subln_g'][j].reshape(1, V_DIM).astype(F32)
            if cache_k is None:
                o = _attn_prompt(q, ktb, vb3, lam_p, sub_g, l)
            else:
                o = _attn_sample(q, cache_k, cache_v, ktb, vb3, lam_p, sub_g, l)
            t = _proj_ln(t, o.reshape(n, D_MODEL), w['attn_w_o'][j], vec(ln_g[l, 1]), vec(ln_b[l, 1]))
        t = ffn(t, l, 1, 2)
    y = t.reshape(nb, length, D_MODEL)
    p_k = kt.transpose(0, 4, 1, 2, 3)
    p_v = v.reshape(nb, length, N_HEADS, V_DIM)
    return y, jnp.stack(new_re), jnp.stack(new_im), p_k, p_v


def kernel(x_prompt, x_sample, cache_k, cache_v, state_ssm_re, state_ssm_im,
           ln_g, ln_b, ffn_w_in, ffn_w_out,
           ssm_a_re, ssm_a_im, ssm_log_dt, ssm_b_re, ssm_b_im, ssm_c_re, ssm_c_im, ssm_d, ssm_w_glu,
           attn_w_q, attn_w_kv, attn_lam, attn_subln_g, attn_w_o):
    w = {'ln_g': ln_g, 'ln_b': ln_b,
         'ffn_w_in': ffn_w_in.astype(BF16), 'ffn_w_out': ffn_w_out.astype(BF16),
         'ssm': [_s5_weights(ssm_a_re[l], ssm_a_im[l], ssm_log_dt[l], ssm_b_re[l], ssm_b_im[l],
                             ssm_c_re[l], ssm_c_im[l]) for l in range(N_A)],
         'ssm_d': ssm_d, 'ssm_w_glu': ssm_w_glu.astype(BF16),
         'attn_w_q': attn_w_q.astype(BF16), 'attn_w_kv': attn_w_kv.astype(BF16), 'attn_lam': attn_lam,
         'attn_subln_g': attn_subln_g, 'attn_w_o': attn_w_o.astype(BF16)}
    b_p, l_p, _ = x_prompt.shape
    pos_p = jnp.arange(l_p, dtype=jnp.int32)
    h0 = jnp.zeros((N_A, b_p, N_GROUPS, P_STATE), F32)
    y_p, p_re, p_im, p_k, p_v = _run_trunk(x_prompt, pos_p, h0, h0, None, None, w)
    past = cache_k.shape[1]
    pos_s = past + jnp.arange(x_sample.shape[1], dtype=jnp.int32)
    y_s, s_re, s_im, s_k, s_v = _run_trunk(x_sample, pos_s, state_ssm_re, state_ssm_im, cache_k, cache_v, w)
    return (y_p, y_s, p_re, p_im, p_k, p_v, s_re, s_im, s_k, s_v)
```

```python
import functools
import math

import jax
import jax.numpy as jnp
from jax import lax
from jax.experimental import pallas as pl
from jax.experimental.pallas import tpu as pltpu

F32 = jnp.float32
BF16 = jnp.bfloat16

D_MODEL = 1024
DEPTH = 2
CHUNK = 64
N_A = DEPTH // 2
GROUP = 16
N_GROUPS = D_MODEL // GROUP
P_STATE = 64
HEAD_DIM = 64
V_DIM = 2 * HEAD_DIM
N_HEADS = D_MODEL // V_DIM
D_FF = ((8 * D_MODEL // 3 + 127) // 128) * 128
ROPE_THETA = 10000.0
ALPHA = (2 * DEPTH) ** 0.25
LN_EPS = 1e-5
RMS_EPS = 1e-5
NEG_INF = -1e30
LOG2_E = math.log2(math.e)

LANES = 128
ROW_TILE = 512
FF_TILE = 256
S5_T = 16
SLAB_GROUPS = LANES // GROUP
N_SLABS = N_GROUPS // SLAB_GROUPS
S5_ROWS = 512
Q_TILE = 512
PAST_TILE = 1024
VMEM_LIMIT = 56 * 1024 * 1024


def _params(n_axes):
    return pltpu.CompilerParams(dimension_semantics=("arbitrary",) * n_axes, vmem_limit_bytes=VMEM_LIMIT)


def _row_tile(n):
    return ROW_TILE if n % ROW_TILE == 0 else n


def _const_spec(shape):
    return pl.BlockSpec(shape, lambda *_: (0,) * len(shape), pipeline_mode=pl.Buffered(1))


def _layer_norm(v, g, b):
    mu = jnp.mean(v, axis=-1, keepdims=True)
    c = v - mu
    var = jnp.mean(c * c, axis=-1, keepdims=True)
    return c * lax.rsqrt(var + LN_EPS) * g + b


def _sigmoid(v):
    return 0.5 * jnp.tanh(0.5 * v) + 0.5


def _gelu_tanh(v):
    return 0.5 * v * (1.0 + jnp.tanh(math.sqrt(2.0 / math.pi) * (v + 0.044715 * (v * v * v))))


def _dot(a, b):
    return jnp.dot(a, b, preferred_element_type=F32)


def _dot_nt(a, b):
    return lax.dot_general(a, b, (((1,), (1,)), ((), ())), preferred_element_type=F32)


def _swiglu_ln(x, win_ref, wout_ref, g, b):
    xb = x.astype(BF16)
    acc = None
    for c in range(D_FF // FF_TILE):
        a = _dot(xb, win_ref[:, pl.ds(c * FF_TILE, FF_TILE)])
        gate = _dot(xb, win_ref[:, pl.ds(D_FF + c * FF_TILE, FF_TILE)])
        h = (a * _sigmoid(a) * gate).astype(BF16)
        part = _dot(h, wout_ref[pl.ds(c * FF_TILE, FF_TILE), :])
        acc = part if acc is None else acc + part
    return _layer_norm(ALPHA * x + 0.5 * acc, g, b)


def _glu_ln(x, ys, d_skip, wg_ref, g, b):
    z = _gelu_tanh(ys + d_skip * x).astype(BF16)
    out = _dot(z, wg_ref[:, pl.ds(0, D_MODEL)])
    gate = _dot(z, wg_ref[:, pl.ds(D_MODEL, D_MODEL)])
    return _layer_norm(ALPHA * x + out * _sigmoid(gate), g, b)


def _q_rope(x, wq_ref, cos, sin, q_ref):
    xb = x.astype(BF16)
    first_half = _first_half_mask(xb.shape[0])
    scale = HEAD_DIM ** -0.5 * LOG2_E
    for j in range(D_MODEL // LANES):
        cols = pl.ds(j * LANES, LANES)
        q = _rope_slab(_dot(xb, wq_ref[:, cols]), cos, sin, first_half)
        q_ref[:, cols] = (q * scale).astype(BF16)


def _ffn_body(*refs, pre, post):
    refs = list(refs)
    x = refs.pop(0)[...]
    if pre == "glu":
        ys_ref, d_ref, wg_ref, g0_ref, b0_ref = refs[:5]
        del refs[:5]
        x = _glu_ln(x, ys_ref[...], d_ref[...], wg_ref, g0_ref[...], b0_ref[...])
    elif pre == "proj":
        o_ref, wo_ref, g0_ref, b0_ref = refs[:4]
        del refs[:4]
        x = _layer_norm(ALPHA * x + _dot(o_ref[...], wo_ref[...]), g0_ref[...], b0_ref[...])
    win_ref, wout_ref, g_ref, b_ref = refs[:4]
    del refs[:4]
    y = _swiglu_ln(x, win_ref, wout_ref, g_ref[...], b_ref[...])
    if post == "q":
        wq_ref, cos_ref, sin_ref, y_ref, q_ref = refs
        _q_rope(y, wq_ref, cos_ref[...], sin_ref[...], q_ref)
    else:
        (y_ref,) = refs
    y_ref[...] = y


def _ffn_ln(x, w_in, w_out, g, b, pre=None, post=None):
    n = x.shape[0]
    tm = _row_tile(n)
    row = pl.BlockSpec((tm, D_MODEL), lambda i: (i, 0))
    vec = _const_spec((1, D_MODEL))
    args, specs = [x], [row]
    if pre is not None and pre[0] == "glu":
        args += list(pre[1:])
        specs += [row, vec, _const_spec((D_MODEL, 2 * D_MODEL)), vec, vec]
    elif pre is not None:
        args += list(pre[1:])
        specs += [row, _const_spec((D_MODEL, D_MODEL)), vec, vec]
    args += [w_in, w_out, g, b]
    specs += [_const_spec((D_MODEL, 2 * D_FF)), _const_spec((D_FF, D_MODEL)), vec, vec]
    out_specs, out_shape = [row], [jax.ShapeDtypeStruct((n, D_MODEL), F32)]
    if post is not None:
        _, w_q, pos, length = post
        table_rows = max(length, tm)
        assert table_rows % length == 0 and table_rows % tm == 0
        cos_t, sin_t = _rope_tables(pos, table_rows)
        n_table_tiles = table_rows // tm
        table = pl.BlockSpec((tm, LANES), lambda i: (i % n_table_tiles, 0))
        args += [w_q, cos_t, sin_t]
        specs += [_const_spec((D_MODEL, D_MODEL)), table, table]
        out_specs.append(row)
        out_shape.append(jax.ShapeDtypeStruct((n, D_MODEL), BF16))
    outs = pl.pallas_call(
        functools.partial(_ffn_body, pre=None if pre is None else pre[0], post=None if post is None else post[0]),
        grid=(n // tm,),
        in_specs=specs,
        out_specs=out_specs,
        out_shape=out_shape,
        compiler_params=_params(1),
        name="ffn_ln",
    )(*args)
    return outs if post is not None else outs[0]


def _s5_weights(a_re, a_im, log_dt, b_re, b_im, c_re, c_im):
    hi = lax.Precision.HIGHEST
    lam_re = jnp.minimum(a_re.astype(F32), -1e-4)
    lam_im = a_im.astype(F32)
    dt = jnp.exp(log_dt.astype(F32))[:, None]
    mag = jnp.exp(lam_re * dt)
    lb_re = mag * jnp.cos(lam_im * dt)
    lb_im = mag * jnp.sin(lam_im * dt)
    den = lam_re * lam_re + lam_im * lam_im
    n_re = lb_re - 1.0
    k_re = (n_re * lam_re + lb_im * lam_im) / den
    k_im = (lb_im * lam_re - n_re * lam_im) / den
    b_re = b_re.astype(F32)
    b_im = b_im.astype(F32)
    bb_re = k_re[..., None] * b_re - k_im[..., None] * b_im
    bb_im = k_re[..., None] * b_im + k_im[..., None] * b_re
    c_re = c_re.astype(F32)
    c_im = c_im.astype(F32)

    def power_step(carry, _):
        pr, pi = carry
        return (pr * lb_re - pi * lb_im, pr * lb_im + pi * lb_re), (pr, pi)

    _, (pw_re, pw_im) = lax.scan(power_step, (jnp.ones_like(lb_re), jnp.zeros_like(lb_re)), None,
                                 length=S5_T + 1)

    cl_re = c_re[None] * pw_re[:, :, None, :] - c_im[None] * pw_im[:, :, None, :]
    cl_im = c_re[None] * pw_im[:, :, None, :] + c_im[None] * pw_re[:, :, None, :]

    taps = (jnp.einsum('ngdp,gpc->ngcd', cl_re[:S5_T], bb_re, precision=hi)
            - jnp.einsum('ngdp,gpc->ngcd', cl_im[:S5_T], bb_im, precision=hi))
    state_w = SLAB_GROUPS * 2 * P_STATE

    lane_of = jnp.arange(LANES)[None, :] % GROUP
    repeat_lanes = (lane_of == jnp.arange(GROUP)[:, None]).astype(F32)
    w_toep = jnp.einsum('ngcd,dl->ngcl', taps, repeat_lanes, precision=hi)
    w_toep = w_toep.reshape(S5_T, N_SLABS, LANES, LANES).transpose(1, 0, 2, 3).astype(BF16)

    rev_re = pw_re[S5_T - 1::-1][:S5_T]
    rev_im = pw_im[S5_T - 1::-1][:S5_T]
    in_re = rev_re[..., None] * bb_re[None] - rev_im[..., None] * bb_im[None]
    in_im = rev_re[..., None] * bb_im[None] + rev_im[..., None] * bb_re[None]
    w_in = jnp.concatenate([in_re, in_im], axis=2)
    w_in = jnp.einsum('ck,sgpk->csgp', jnp.eye(GROUP, dtype=F32), w_in, precision=hi).transpose(1, 2, 0, 3)
    w_in = w_in.reshape(S5_T, N_SLABS, LANES, 2 * P_STATE).transpose(1, 0, 2, 3).astype(BF16)

    w_out = jnp.concatenate([cl_re[1:], -cl_im[1:]], axis=3)
    w_out = w_out.reshape(S5_T, N_SLABS, SLAB_GROUPS, GROUP, 2 * P_STATE).transpose(1, 0, 4, 2, 3)
    w_out = w_out.reshape(N_SLABS, S5_T, 2 * P_STATE, LANES).astype(BF16)

    lt_re, lt_im = pw_re[S5_T], pw_im[S5_T]
    coef = jnp.stack([jnp.concatenate([lt_re, lt_re], -1),
                      jnp.concatenate([-lt_im, lt_im], -1),
                      jnp.concatenate([lt_im, -lt_im], -1)], axis=0)
    coef = coef.reshape(3, N_SLABS, state_w).transpose(1, 0, 2)
    coef = jnp.concatenate([coef, jnp.zeros((N_SLABS, 5, state_w), F32)], axis=1)
    return w_toep, w_in, w_out, coef


def _s5_expand_weights(toep_ref, cin_ref, cout_ref, wt_ref, win_ref, wout_ref):
    group_of_row = lax.broadcasted_iota(jnp.int32, (LANES, LANES), 0) // GROUP
    group_of_col = lax.broadcasted_iota(jnp.int32, (LANES, LANES), 1) // GROUP
    zero = jnp.zeros((LANES, LANES), BF16)
    taps = [jnp.where(group_of_row == group_of_col, toep_ref[n], zero) for n in range(S5_T)]
    for s in range(S5_T):
        rows = pl.ds(s * LANES, LANES)
        for t in range(S5_T):
            wt_ref[rows, pl.ds(t * LANES, LANES)] = taps[t - s] if t >= s else zero
        tile_in = cin_ref[s]
        tile_out = cout_ref[s]
        for j in range(SLAB_GROUPS):
            win_ref[rows, pl.ds(j * LANES, LANES)] = jnp.where(group_of_row == j, tile_in, zero)
            wout_ref[pl.ds(j * LANES, LANES), rows] = jnp.where(group_of_col == j, tile_out, zero)


def _s5_body(x_ref, toep_ref, cin_ref, cout_ref, coef_ref, h0_ref, y_ref, hfin_ref,
             wt_ref, win_ref, wout_ref, s_ref, hst_ref, *, rows, n_chunks, bt):
    @pl.when(pl.program_id(1) == 0)
    def _():
        _s5_expand_weights(toep_ref, cin_ref, cout_ref, wt_ref, win_ref, wout_ref)

    xcat = jnp.concatenate([x_ref[pl.ds(s, rows, stride=S5_T), :].astype(BF16) for s in range(S5_T)], axis=1)
    s_all = _dot(xcat, win_ref[...])
    for j in range(SLAB_GROUPS):
        sj = s_all[:, j * LANES:(j + 1) * LANES]
        s_ref[j] = sj
        s_ref[SLAB_GROUPS + j] = pltpu.roll(sj, P_STATE, 1)
    coef = coef_ref[...]
    h0 = h0_ref[...]

    def lanes(v, j):
        return v[:, j * LANES:(j + 1) * LANES]

    def step(k, carry):
        hs, hts = carry
        idx = pl.ds(k, bt, stride=n_chunks)
        new_h, new_ht = [], []
        for j in range(SLAB_GROUPS):
            hst_ref[j, idx, :] = hs[j]
            c_self, c_swap, c_swap_t = lanes(coef[0:1], j), lanes(coef[1:2], j), lanes(coef[2:3], j)
            new_h.append(c_self * hs[j] + c_swap * hts[j] + s_ref[j, idx, :])
            new_ht.append(c_self * hts[j] + c_swap_t * hs[j] + s_ref[SLAB_GROUPS + j, idx, :])
        return tuple(new_h), tuple(new_ht)

    init = (tuple(lanes(h0[0], j) for j in range(SLAB_GROUPS)), tuple(lanes(h0[1], j) for j in range(SLAB_GROUPS)))
    carry = init
    for k in range(n_chunks):
        carry = step(k, carry)
    hs, _ = carry
    hfin_ref[...] = jnp.concatenate(hs, axis=1)
    hst = jnp.concatenate([hst_ref[j] for j in range(SLAB_GROUPS)], axis=1).astype(BF16)
    for t in range(0, S5_T, 2):
        cols = pl.ds(t * LANES, 2 * LANES)
        k_used = (t + 2) * LANES
        y = _dot(xcat[:, :k_used], wt_ref[pl.ds(0, k_used), cols]) + _dot(hst, wout_ref[:, cols])
        y_ref[pl.ds(t, rows, stride=S5_T), :] = y[:, :LANES]
        y_ref[pl.ds(t + 1, rows, stride=S5_T), :] = y[:, LANES:]


def _s5_batch_tile(nb, n_chunks):
    bt = nb
    while bt * n_chunks > S5_ROWS and bt % 2 == 0:
        bt //= 2
    return bt


def _s5_core(u, nb, h0_re, h0_im, weights):
    w_toep, w_in, w_out, coef = weights
    n = u.shape[0]
    length = n // nb
    n_chunks = length // S5_T
    bt = _s5_batch_tile(nb, n_chunks)
    n_bt = nb // bt
    rows = bt * n_chunks
    wide = S5_T * LANES
    sw = SLAB_GROUPS * 2 * P_STATE

    def slab_state(re, im):
        v = jnp.concatenate([re, im], axis=-1).astype(F32)
        return v.reshape(n_bt, bt, N_SLABS, sw).transpose(2, 0, 1, 3)

    h0 = jnp.stack([slab_state(h0_re, h0_im), slab_state(h0_im, h0_re)], axis=2)

    def per_slab(shape):
        return pl.BlockSpec((None,) + shape, lambda s, i: (s,) + (0,) * len(shape), pipeline_mode=pl.Buffered(1))

    tokens = pl.BlockSpec((bt * length, LANES), lambda s, i: (i, s))
    y, hfin = pl.pallas_call(
        functools.partial(_s5_body, rows=rows, n_chunks=n_chunks, bt=bt),
        grid=(N_SLABS, n_bt),
        in_specs=[tokens, per_slab((S5_T, LANES, LANES)), per_slab((S5_T, LANES, 2 * P_STATE)),
                  per_slab((S5_T, 2 * P_STATE, LANES)), per_slab((8, sw)),
                  pl.BlockSpec((None, None, 2, bt, sw), lambda s, i: (s, i, 0, 0, 0))],
        out_specs=[tokens, pl.BlockSpec((None, None, bt, sw), lambda s, i: (s, i, 0, 0))],
        out_shape=[jax.ShapeDtypeStruct((n, D_MODEL), F32), jax.ShapeDtypeStruct((N_SLABS, n_bt, bt, sw), F32)],
        scratch_shapes=[pltpu.VMEM((wide, wide), BF16), pltpu.VMEM((wide, sw), BF16), pltpu.VMEM((sw, wide), BF16),
                        pltpu.VMEM((2 * SLAB_GROUPS, rows, LANES), F32), pltpu.VMEM((SLAB_GROUPS, rows, LANES), F32)],
        compiler_params=_params(2),
        name="s5_core",
    )(u, w_toep, w_in, w_out, coef, h0)
    hfin = hfin.transpose(1, 2, 0, 3).reshape(nb, N_GROUPS, 2 * P_STATE)
    return y, hfin[..., :P_STATE], hfin[..., P_STATE:]


def _rope_tables(pos, rows):
    half = HEAD_DIM // 2
    inv = ROPE_THETA ** (-jnp.arange(half, dtype=F32) / half)
    ang = pos.astype(F32)[:, None] * inv[None, :]
    cos = jnp.cos(ang)
    sin = jnp.sin(ang)
    cos_t = jnp.concatenate([cos, cos, cos, cos], axis=-1)
    sin_t = jnp.concatenate([-sin, sin, -sin, sin], axis=-1)
    reps = rows // pos.shape[0]
    return jnp.tile(cos_t, (reps, 1)), jnp.tile(sin_t, (reps, 1))


def _rope_slab(v, cos, sin, first_half):
    partner = jnp.where(first_half, pltpu.roll(v, LANES - HEAD_DIM // 2, 1), pltpu.roll(v, HEAD_DIM // 2, 1))
    return v * cos + partner * sin


def _first_half_mask(rows):
    lane = lax.broadcasted_iota(jnp.int32, (rows, LANES), 1)
    return (lane % HEAD_DIM) < (HEAD_DIM // 2)


def _kv_rope_body(x_ref, wk_ref, wv_ref, cos_ref, sin_ref, kt_ref, ktb_ref, v_ref, vb_ref):
    xb = x_ref[...].astype(BF16)
    tm = xb.shape[0]
    half = HEAD_DIM // 2
    cos = cos_ref[...]
    sin = sin_ref[...]
    kt = _dot_nt(wk_ref[...], xb)
    for h in range(N_HEADS):
        for mp in range(2):
            lo = (2 * h + mp) * HEAD_DIM
            x1 = kt[lo:lo + half]
            x2 = kt[lo + half:lo + HEAD_DIM]
            r1 = x1 * cos - x2 * sin
            r2 = x1 * sin + x2 * cos
            kt_ref[h, mp, pl.ds(0, half), :] = r1
            kt_ref[h, mp, pl.ds(half, half), :] = r2
            ktb_ref[h, pl.ds(mp * HEAD_DIM, half), :] = r1.astype(BF16)
            ktb_ref[h, pl.ds(mp * HEAD_DIM + half, half), :] = r2.astype(BF16)
    for h in range(N_HEADS):
        cols = pl.ds(h * V_DIM, V_DIM)
        v = _dot(xb, wv_ref[:, cols])
        v_ref[pl.ds(h, tm, stride=N_HEADS), :] = v
        vb_ref[:, cols] = v.astype(BF16)


def _kv_rope(x, nb, w_kv, pos):
    n = x.shape[0]
    length = n // nb
    tm = min(ROW_TILE, length)
    tiles = length // tm
    half = HEAD_DIM // 2
    inv = ROPE_THETA ** (-jnp.arange(half, dtype=F32) / half)
    ang = inv[:, None] * pos.astype(F32)[None, :]
    wk_t = w_kv[:, :D_MODEL].T
    w_v = w_kv[:, D_MODEL:]
    table = pl.BlockSpec((half, tm), lambda i: (0, i % tiles))
    return pl.pallas_call(
        _kv_rope_body,
        grid=(n // tm,),
        in_specs=[pl.BlockSpec((tm, D_MODEL), lambda i: (i, 0)), _const_spec((D_MODEL, D_MODEL)),
                  _const_spec((D_MODEL, D_MODEL)), table, table],
        out_specs=[pl.BlockSpec((None, N_HEADS, 2, HEAD_DIM, tm), lambda i: (i // tiles, 0, 0, 0, i % tiles)),
                   pl.BlockSpec((None, N_HEADS, V_DIM, tm), lambda i: (i // tiles, 0, 0, i % tiles)),
                   pl.BlockSpec((tm * N_HEADS, V_DIM), lambda i: (i, 0)),
                   pl.BlockSpec((tm, D_MODEL), lambda i: (i, 0))],
        out_shape=[jax.ShapeDtypeStruct((nb, N_HEADS, 2, HEAD_DIM, length), F32),
                   jax.ShapeDtypeStruct((nb, N_HEADS, V_DIM, length), BF16),
                   jax.ShapeDtypeStruct((n * N_HEADS, V_DIM), F32),
                   jax.ShapeDtypeStruct((n, D_MODEL), BF16)],
        compiler_params=_params(1),
        name="kv_rope",
    )(x, wk_t, w_v, jnp.cos(ang), jnp.sin(ang))


def _lam_value(lam_ref, layer_idx):
    lv = lam_ref[...]
    s01 = jnp.sum(lv[0:1, :] * lv[1:2, :], axis=-1, keepdims=True)
    s23 = jnp.sum(lv[2:3, :] * lv[3:4, :], axis=-1, keepdims=True)
    lam_init = 0.8 - 0.6 * math.exp(-0.3 * layer_idx)
    return jnp.exp(s01) - jnp.exp(s23) + lam_init, lam_init


def _diff_attend(q, segments, lam, lam_init, subln_g):
    lane = lax.broadcasted_iota(jnp.int32, q.shape, 1)
    zero = jnp.zeros_like(q)
    q1 = jnp.where(lane < HEAD_DIM, q, zero)
    q2 = jnp.where(lane >= HEAD_DIM, q, zero)
    maps = []
    for qm in (q1, q2):
        scores = []
        for kt, _, mask in segments:
            s = _dot(qm, kt)
            scores.append(s if mask is None else jnp.where(mask, s, NEG_INF))
        m = functools.reduce(jnp.maximum, [jnp.max(s, axis=-1, keepdims=True) for s in scores])
        es = [jnp.exp2(s - m) for s in scores]
        denom = functools.reduce(jnp.add, [jnp.sum(e, axis=-1, keepdims=True) for e in es])
        acc = functools.reduce(jnp.add, [_dot(e.astype(BF16), v) for e, (_, v, _) in zip(es, segments)])
        maps.append(acc / denom)
    out = maps[0] - lam * maps[1]
    ms = jnp.mean(out * out, axis=-1, keepdims=True)
    return out * lax.rsqrt(ms + RMS_EPS) * subln_g * (1.0 - lam_init)


def _attn_prompt_body(q_ref, kt_ref, v_ref, lam_ref, g_ref, o_ref, *, length, tq, layer_idx):
    lam, lam_init = _lam_value(lam_ref, layer_idx)
    g = g_ref[...]
    r = lax.broadcasted_iota(jnp.int32, (tq, tq), 0) // CHUNK
    c = lax.broadcasted_iota(jnp.int32, (tq, tq), 1) // CHUNK
    diag_mask = c <= r
    for i in range(length // tq):
        rows = pl.ds(i * tq, tq)
        segments = []
        if i > 0:
            segments.append((kt_ref[:, pl.ds(0, i * tq)], v_ref[pl.ds(0, i * tq), :], None))
        segments.append((kt_ref[:, rows], v_ref[rows, :], diag_mask))
        o_ref[rows, :] = _diff_attend(q_ref[rows, :], segments, lam, lam_init, g).astype(BF16)


def _attn_prompt(q, ktb, vb, attn_lam, subln_g, layer_idx):
    nb, length, _ = q.shape
    tq = min(Q_TILE, length)
    head = pl.BlockSpec((None, length, V_DIM), lambda b, h: (b, 0, h))
    head_t = pl.BlockSpec((None, None, V_DIM, length), lambda b, h: (b, h, 0, 0))
    return pl.pallas_call(
        functools.partial(_attn_prompt_body, length=length, tq=tq, layer_idx=layer_idx),
        grid=(nb, N_HEADS),
        in_specs=[head, head_t, head, _const_spec((4, HEAD_DIM)), _const_spec((1, V_DIM))],
        out_specs=head,
        out_shape=jax.ShapeDtypeStruct((nb, length, D_MODEL), BF16),
        compiler_params=_params(2),
        name="attn_prompt",
    )(q, ktb, vb, attn_lam, subln_g)


def _attn_sample_body(q_ref, kp_ref, vp_ref, kt_ref, v_ref, lam_ref, g_ref, o_ref, m_ref, l_ref, acc_ref, *, layer_idx, tp):
    pt = pl.program_id(1)
    n_q = q_ref.shape[0]
    zero = jnp.zeros((n_q, HEAD_DIM), BF16)

    def head_scores(h, k1, k2):
        q1 = q_ref[:, h * V_DIM:h * V_DIM + HEAD_DIM]
        q2 = q_ref[:, h * V_DIM + HEAD_DIM:(h + 1) * V_DIM]
        return _dot(jnp.concatenate([q1, zero], axis=0), k1) + _dot(jnp.concatenate([zero, q2], axis=0), k2)

    def accumulate(keys_of, values_of, first):
        s = jnp.concatenate([head_scores(h, *keys_of(h)) for h in range(N_HEADS)], axis=0)
        m_tile = jnp.max(s, axis=-1, keepdims=True)
        if first:
            m_new = m_tile
        else:
            m_old = m_ref[...]
            m_new = jnp.maximum(m_old, m_tile)
            alpha = jnp.exp2(m_old - m_new)
        e = jnp.exp2(s - m_new)
        l_tile = jnp.sum(e, axis=-1, keepdims=True)
        eb = e.astype(BF16)
        pv = jnp.concatenate([_dot(eb[2 * n_q * h:2 * n_q * (h + 1)], values_of(h)) for h in range(N_HEADS)], axis=0)
        m_ref[...] = m_new
        l_ref[...] = l_tile if first else alpha * l_ref[...] + l_tile
        acc_ref[...] = pv if first else alpha * acc_ref[...] + pv

    @pl.when(pt == 0)
    def _():
        accumulate(lambda h: (kt_ref[h, pl.ds(0, HEAD_DIM), :], kt_ref[h, pl.ds(HEAD_DIM, HEAD_DIM), :]),
                   lambda h: v_ref[:, h * V_DIM:(h + 1) * V_DIM], True)

    accumulate(lambda h: (kp_ref[h, 0].astype(BF16), kp_ref[h, 1].astype(BF16)),
               lambda h: vp_ref[pl.ds(h, tp, stride=N_HEADS), :].astype(BF16), False)

    @pl.when(pt == pl.num_programs(1) - 1)
    def _():
        lam, lam_init = _lam_value(lam_ref, layer_idx)
        g = g_ref[...]
        for h in range(N_HEADS):
            r1 = pl.ds(2 * n_q * h, n_q)
            r2 = pl.ds(2 * n_q * h + n_q, n_q)
            out = acc_ref[r1, :] / l_ref[r1, :] - acc_ref[r2, :] * (lam / l_ref[r2, :])
            ms = jnp.mean(out * out, axis=-1, keepdims=True)
            o_ref[:, h * V_DIM:(h + 1) * V_DIM] = (out * lax.rsqrt(ms + RMS_EPS) * g * (1.0 - lam_init)).astype(BF16)


def _attn_sample(q, cache_k, cache_v, ktb, vb, attn_lam, subln_g, layer_idx):
    nb, length, _ = q.shape
    past = cache_k.shape[1]
    assert past % CHUNK == 0 and length <= CHUNK
    tp = PAST_TILE if past % PAST_TILE == 0 else past
    new = pl.BlockSpec((None, length, D_MODEL), lambda b, p: (b, 0, 0))
    hm = 2 * N_HEADS
    return pl.pallas_call(
        functools.partial(_attn_sample_body, layer_idx=layer_idx, tp=tp),
        grid=(nb, past // tp),
        in_specs=[new,
                  pl.BlockSpec((None, N_HEADS, 2, HEAD_DIM, tp), lambda b, p: (b, 0, 0, 0, p)),
                  pl.BlockSpec((None, tp * N_HEADS, V_DIM), lambda b, p: (b, p, 0)),
                  pl.BlockSpec((None, N_HEADS, V_DIM, length), lambda b, p: (b, 0, 0, 0)),
                  new, _const_spec((4, HEAD_DIM)), _const_spec((1, V_DIM))],
        out_specs=new,
        out_shape=jax.ShapeDtypeStruct((nb, length, D_MODEL), BF16),
        scratch_shapes=[pltpu.VMEM((hm * length, 1), F32), pltpu.VMEM((hm * length, 1), F32),
                        pltpu.VMEM((hm * length, V_DIM), F32)],
        compiler_params=_params(2),
        name="attn_sample",
    )(q, cache_k.transpose(0, 2, 3, 4, 1), cache_v.reshape(nb, past * N_HEADS, V_DIM), ktb, vb, attn_lam, subln_g)


def _run_trunk(x, pos, h0_re, h0_im, cache_k, cache_v, w):
    nb, length, _ = x.shape
    n = nb * length
    assert length % S5_T == 0
    ln_g, ln_b = w['ln_g'], w['ln_b']

    def vec(v):
        return v.reshape(1, D_MODEL).astype(F32)

    def ffn(t, l, j, ln_idx, **fused):
        return _ffn_ln(t, w['ffn_w_in'][l, j], w['ffn_w_out'][l, j], vec(ln_g[l, ln_idx]), vec(ln_b[l, ln_idx]), **fused)

    t = x.reshape(n, D_MODEL)
    new_re, new_im = [], []
    kt = v = None
    for l in range(DEPTH):
        if l == N_A:
            kt, ktb, v, vb = _kv_rope(t, nb, w['attn_w_kv'], pos)
        if l < N_A:
            t = ffn(t, l, 0, 0)
            ys, hr, hi = _s5_core(t, nb, h0_re[l], h0_im[l], w['ssm'][l])
            new_re.append(hr)
            new_im.append(hi)
            mixer = ("glu", ys, vec(w['ssm_d'][l]), w['ssm_w_glu'][l], vec(ln_g[l, 1]), vec(ln_b[l, 1]))
        else:
            j = l - N_A
            t, q = ffn(t, l, 0, 0, post=("q", w['attn_w_q'][j], pos, length))
            q = q.reshape(nb, length, D_MODEL)
            vb3 = vb.reshape(nb, length, D_MODEL)
            lam_p = w['attn_lam'][j].astype(F32)
            sub_g = w['attn_subln_g'][j].reshape(1, V_DIM).astype(F32)
            if cache_k is None:
                o = _attn_prompt(q, ktb, vb3, lam_p, sub_g, l)
            else:
                o = _attn_sample(q, cache_k, cache_v, ktb, vb3, lam_p, sub_g, l)
            mixer = ("proj", o.reshape(n, D_MODEL), w['attn_w_o'][j], vec(ln_g[l, 1]), vec(ln_b[l, 1]))
        t = ffn(t, l, 1, 2, pre=mixer)
    y = t.reshape(nb, length, D_MODEL)
    p_k = kt.transpose(0, 4, 1, 2, 3)
    p_v = v.reshape(nb, length, N_HEADS, V_DIM)
    return y, jnp.stack(new_re), jnp.stack(new_im), p_k, p_v


def kernel(x_prompt, x_sample, cache_k, cache_v, state_ssm_re, state_ssm_im,
           ln_g, ln_b, ffn_w_in, ffn_w_out,
           ssm_a_re, ssm_a_im, ssm_log_dt, ssm_b_re, ssm_b_im, ssm_c_re, ssm_c_im, ssm_d, ssm_w_glu,
           attn_w_q, attn_w_kv, attn_lam, attn_subln_g, attn_w_o):
    w = {'ln_g': ln_g, 'ln_b': ln_b,
         'ffn_w_in': ffn_w_in.astype(BF16), 'ffn_w_out': ffn_w_out.astype(BF16),
         'ssm': [_s5_weights(ssm_a_re[l], ssm_a_im[l], ssm_log_dt[l], ssm_b_re[l], ssm_b_im[l],
                             ssm_c_re[l], ssm_c_im[l]) for l in range(N_A)],
         'ssm_d': ssm_d, 'ssm_w_glu': ssm_w_glu.astype(BF16),
         'attn_w_q': attn_w_q.astype(BF16), 'attn_w_kv': attn_w_kv.astype(BF16), 'attn_lam': attn_lam,
         'attn_subln_g': attn_subln_g, 'attn_w_o': attn_w_o.astype(BF16)}
    b_p, l_p, _ = x_prompt.shape
    pos_p = jnp.arange(l_p, dtype=jnp.int32)
    h0 = jnp.zeros((N_A, b_p, N_GROUPS, P_STATE), F32)
    y_p, p_re, p_im, p_k, p_v = _run_trunk(x_prompt, pos_p, h0, h0, None, None, w)
    past = cache_k.shape[1]
    pos_s = past + jnp.arange(x_sample.shape[1], dtype=jnp.int32)
    y_s, s_re, s_im, s_k, s_v = _run_trunk(x_sample, pos_s, state_ssm_re, state_ssm_im, cache_k, cache_v, w)
    return (y_p, y_s, p_re, p_im, p_k, p_v, s_re, s_im, s_k, s_v)
```

```python
import functools
import math

import jax
import jax.numpy as jnp
from jax import lax
from jax.experimental import pallas as pl
from jax.experimental.pallas import tpu as pltpu

F32 = jnp.float32
BF16 = jnp.bfloat16

D_MODEL = 1024
DEPTH = 2
CHUNK = 64
N_A = DEPTH // 2
GROUP = 16
N_GROUPS = D_MODEL // GROUP
P_STATE = 64
HEAD_DIM = 64
V_DIM = 2 * HEAD_DIM
N_HEADS = D_MODEL // V_DIM
D_FF = ((8 * D_MODEL // 3 + 127) // 128) * 128
ROPE_THETA = 10000.0
ALPHA = (2 * DEPTH) ** 0.25
LN_EPS = 1e-5
RMS_EPS = 1e-5
NEG_INF = -1e30
LOG2_E = math.log2(math.e)

LANES = 128
ROW_TILE = 512
FF_TILE = 256
S5_T = 16
SLAB_GROUPS = LANES // GROUP
N_SLABS = N_GROUPS // SLAB_GROUPS
S5_ROWS = 512
Q_TILE = 512
ATTN_HEADS = 2
PAST_TILE = 1024
VMEM_LIMIT = 56 * 1024 * 1024


def _params(n_axes):
    return pltpu.CompilerParams(dimension_semantics=("arbitrary",) * n_axes, vmem_limit_bytes=VMEM_LIMIT)


def _row_tile(n):
    return ROW_TILE if n % ROW_TILE == 0 else n


def _const_spec(shape):
    return pl.BlockSpec(shape, lambda *_: (0,) * len(shape), pipeline_mode=pl.Buffered(1))


def _layer_norm(v, g, b):
    mu = jnp.mean(v, axis=-1, keepdims=True)
    c = v - mu
    var = jnp.mean(c * c, axis=-1, keepdims=True)
    return c * lax.rsqrt(var + LN_EPS) * g + b


def _sigmoid(v):
    return 0.5 * jnp.tanh(0.5 * v) + 0.5


def _gelu_tanh(v):
    return 0.5 * v * (1.0 + jnp.tanh(math.sqrt(2.0 / math.pi) * (v + 0.044715 * (v * v * v))))


def _dot(a, b):
    return jnp.dot(a, b, preferred_element_type=F32)


def _dot_nt(a, b):
    return lax.dot_general(a, b, (((1,), (1,)), ((), ())), preferred_element_type=F32)


def _swiglu_ln(x, win_ref, wout_ref, g, b):
    xb = x.astype(BF16)
    acc = None
    for c in range(D_FF // FF_TILE):
        a = _dot(xb, win_ref[:, pl.ds(c * FF_TILE, FF_TILE)])
        gate = _dot(xb, win_ref[:, pl.ds(D_FF + c * FF_TILE, FF_TILE)])
        h = (a * _sigmoid(a) * gate).astype(BF16)
        part = _dot(h, wout_ref[pl.ds(c * FF_TILE, FF_TILE), :])
        acc = part if acc is None else acc + part
    return _layer_norm(ALPHA * x + 0.5 * acc, g, b)


def _glu_ln(x, ys, d_skip, wg_ref, g, b):
    z = _gelu_tanh(ys + d_skip * x).astype(BF16)
    out = _dot(z, wg_ref[:, pl.ds(0, D_MODEL)])
    gate = _dot(z, wg_ref[:, pl.ds(D_MODEL, D_MODEL)])
    return _layer_norm(ALPHA * x + out * _sigmoid(gate), g, b)


def _q_rope(x, wq_ref, cos, sin, q_ref):
    xb = x.astype(BF16)
    first_half = _first_half_mask(xb.shape[0])
    scale = HEAD_DIM ** -0.5 * LOG2_E
    for j in range(D_MODEL // LANES):
        cols = pl.ds(j * LANES, LANES)
        q = _rope_slab(_dot(xb, wq_ref[:, cols]), cos, sin, first_half)
        q_ref[:, cols] = (q * scale).astype(BF16)


def _ffn_body(*refs, pre, post):
    refs = list(refs)
    x = refs.pop(0)[...]
    if pre == "glu":
        ys_ref, d_ref, wg_ref, g0_ref, b0_ref = refs[:5]
        del refs[:5]
        x = _glu_ln(x, ys_ref[...], d_ref[...], wg_ref, g0_ref[...], b0_ref[...])
    elif pre == "proj":
        o_ref, wo_ref, g0_ref, b0_ref = refs[:4]
        del refs[:4]
        x = _layer_norm(ALPHA * x + _dot(o_ref[...], wo_ref[...]), g0_ref[...], b0_ref[...])
    win_ref, wout_ref, g_ref, b_ref = refs[:4]
    del refs[:4]
    y = _swiglu_ln(x, win_ref, wout_ref, g_ref[...], b_ref[...])
    if post == "q":
        wq_ref, cos_ref, sin_ref, y_ref, q_ref = refs
        _q_rope(y, wq_ref, cos_ref[...], sin_ref[...], q_ref)
    else:
        (y_ref,) = refs
    y_ref[...] = y


def _ffn_ln(x, w_in, w_out, g, b, pre=None, post=None):
    n = x.shape[0]
    tm = _row_tile(n)
    row = pl.BlockSpec((tm, D_MODEL), lambda i: (i, 0))
    vec = _const_spec((1, D_MODEL))
    args, specs = [x], [row]
    if pre is not None and pre[0] == "glu":
        args += list(pre[1:])
        specs += [row, vec, _const_spec((D_MODEL, 2 * D_MODEL)), vec, vec]
    elif pre is not None:
        args += list(pre[1:])
        specs += [row, _const_spec((D_MODEL, D_MODEL)), vec, vec]
    args += [w_in, w_out, g, b]
    specs += [_const_spec((D_MODEL, 2 * D_FF)), _const_spec((D_FF, D_MODEL)), vec, vec]
    out_specs, out_shape = [row], [jax.ShapeDtypeStruct((n, D_MODEL), F32)]
    if post is not None:
        _, w_q, pos, length = post
        table_rows = max(length, tm)
        assert table_rows % length == 0 and table_rows % tm == 0
        cos_t, sin_t = _rope_tables(pos, table_rows)
        n_table_tiles = table_rows // tm
        table = pl.BlockSpec((tm, LANES), lambda i: (i % n_table_tiles, 0))
        args += [w_q, cos_t, sin_t]
        specs += [_const_spec((D_MODEL, D_MODEL)), table, table]
        out_specs.append(row)
        out_shape.append(jax.ShapeDtypeStruct((n, D_MODEL), BF16))
    outs = pl.pallas_call(
        functools.partial(_ffn_body, pre=None if pre is None else pre[0], post=None if post is None else post[0]),
        grid=(n // tm,),
        in_specs=specs,
        out_specs=out_specs,
        out_shape=out_shape,
        compiler_params=_params(1),
        name="ffn_ln",
    )(*args)
    return outs if post is not None else outs[0]


def _s5_weights(a_re, a_im, log_dt, b_re, b_im, c_re, c_im):
    hi = lax.Precision.HIGHEST
    lam_re = jnp.minimum(a_re.astype(F32), -1e-4)
    lam_im = a_im.astype(F32)
    dt = jnp.exp(log_dt.astype(F32))[:, None]
    mag = jnp.exp(lam_re * dt)
    lb_re = mag * jnp.cos(lam_im * dt)
    lb_im = mag * jnp.sin(lam_im * dt)
    den = lam_re * lam_re + lam_im * lam_im
    n_re = lb_re - 1.0
    k_re = (n_re * lam_re + lb_im * lam_im) / den
    k_im = (lb_im * lam_re - n_re * lam_im) / den
    b_re = b_re.astype(F32)
    b_im = b_im.astype(F32)
    bb_re = k_re[..., None] * b_re - k_im[..., None] * b_im
    bb_im = k_re[..., None] * b_im + k_im[..., None] * b_re
    c_re = c_re.astype(F32)
    c_im = c_im.astype(F32)

    def power_step(carry, _):
        pr, pi = carry
        return (pr * lb_re - pi * lb_im, pr * lb_im + pi * lb_re), (pr, pi)

    _, (pw_re, pw_im) = lax.scan(power_step, (jnp.ones_like(lb_re), jnp.zeros_like(lb_re)), None,
                                 length=S5_T + 1)

    cl_re = c_re[None] * pw_re[:, :, None, :] - c_im[None] * pw_im[:, :, None, :]
    cl_im = c_re[None] * pw_im[:, :, None, :] + c_im[None] * pw_re[:, :, None, :]

    taps = (jnp.einsum('ngdp,gpc->ngcd', cl_re[:S5_T], bb_re, precision=hi)
            - jnp.einsum('ngdp,gpc->ngcd', cl_im[:S5_T], bb_im, precision=hi))
    state_w = SLAB_GROUPS * 2 * P_STATE

    lane_of = jnp.arange(LANES)[None, :] % GROUP
    repeat_lanes = (lane_of == jnp.arange(GROUP)[:, None]).astype(F32)
    w_toep = jnp.einsum('ngcd,dl->ngcl', taps, repeat_lanes, precision=hi)
    w_toep = w_toep.reshape(S5_T, N_SLABS, LANES, LANES).transpose(1, 0, 2, 3).astype(BF16)

    rev_re = pw_re[S5_T - 1::-1][:S5_T]
    rev_im = pw_im[S5_T - 1::-1][:S5_T]
    in_re = rev_re[..., None] * bb_re[None] - rev_im[..., None] * bb_im[None]
    in_im = rev_re[..., None] * bb_im[None] + rev_im[..., None] * bb_re[None]
    w_in = jnp.concatenate([in_re, in_im], axis=2)
    w_in = jnp.einsum('ck,sgpk->csgp', jnp.eye(GROUP, dtype=F32), w_in, precision=hi).transpose(1, 2, 0, 3)
    w_in = w_in.reshape(S5_T, N_SLABS, LANES, 2 * P_STATE).transpose(1, 0, 2, 3).astype(BF16)

    w_out = jnp.concatenate([cl_re[1:], -cl_im[1:]], axis=3)
    w_out = w_out.reshape(S5_T, N_SLABS, SLAB_GROUPS, GROUP, 2 * P_STATE).transpose(1, 0, 4, 2, 3)
    w_out = w_out.reshape(N_SLABS, S5_T, 2 * P_STATE, LANES).astype(BF16)

    lt_re, lt_im = pw_re[S5_T], pw_im[S5_T]
    coef = jnp.stack([jnp.concatenate([lt_re, lt_re], -1),
                      jnp.concatenate([-lt_im, lt_im], -1),
                      jnp.concatenate([lt_im, -lt_im], -1)], axis=0)
    coef = coef.reshape(3, N_SLABS, state_w).transpose(1, 0, 2)
    coef = jnp.concatenate([coef, jnp.zeros((N_SLABS, 5, state_w), F32)], axis=1)
    return w_toep, w_in, w_out, coef


def _s5_expand_weights(toep_ref, cin_ref, cout_ref, wt_ref, win_ref, wout_ref):
    group_of_row = lax.broadcasted_iota(jnp.int32, (LANES, LANES), 0) // GROUP
    group_of_col = lax.broadcasted_iota(jnp.int32, (LANES, LANES), 1) // GROUP
    zero = jnp.zeros((LANES, LANES), BF16)
    taps = [jnp.where(group_of_row == group_of_col, toep_ref[n], zero) for n in range(S5_T)]
    for s in range(S5_T):
        rows = pl.ds(s * LANES, LANES)
        for t in range(S5_T):
            wt_ref[rows, pl.ds(t * LANES, LANES)] = taps[t - s] if t >= s else zero
        tile_in = cin_ref[s]
        tile_out = cout_ref[s]
        for j in range(SLAB_GROUPS):
            win_ref[rows, pl.ds(j * LANES, LANES)] = jnp.where(group_of_row == j, tile_in, zero)
            wout_ref[pl.ds(j * LANES, LANES), rows] = jnp.where(group_of_col == j, tile_out, zero)


def _s5_body(x_ref, toep_ref, cin_ref, cout_ref, coef_ref, h0_ref, y_ref, hfin_ref,
             wt_ref, win_ref, wout_ref, s_ref, hst_ref, *, rows, n_chunks, bt):
    @pl.when(pl.program_id(1) == 0)
    def _():
        _s5_expand_weights(toep_ref, cin_ref, cout_ref, wt_ref, win_ref, wout_ref)

    xcat = jnp.concatenate([x_ref[pl.ds(s, rows, stride=S5_T), :].astype(BF16) for s in range(S5_T)], axis=1)
    s_all = _dot(xcat, win_ref[...])
    for j in range(SLAB_GROUPS):
        sj = s_all[:, j * LANES:(j + 1) * LANES]
        s_ref[j] = sj
        s_ref[SLAB_GROUPS + j] = pltpu.roll(sj, P_STATE, 1)
    coef = coef_ref[...]
    h0 = h0_ref[...]

    def lanes(v, j):
        return v[:, j * LANES:(j + 1) * LANES]

    def step(k, carry):
        hs, hts = carry
        idx = pl.ds(k, bt, stride=n_chunks)
        new_h, new_ht = [], []
        for j in range(SLAB_GROUPS):
            hst_ref[j, idx, :] = hs[j]
            c_self, c_swap, c_swap_t = lanes(coef[0:1], j), lanes(coef[1:2], j), lanes(coef[2:3], j)
            new_h.append(c_self * hs[j] + c_swap * hts[j] + s_ref[j, idx, :])
            new_ht.append(c_self * hts[j] + c_swap_t * hs[j] + s_ref[SLAB_GROUPS + j, idx, :])
        return tuple(new_h), tuple(new_ht)

    init = (tuple(lanes(h0[0], j) for j in range(SLAB_GROUPS)), tuple(lanes(h0[1], j) for j in range(SLAB_GROUPS)))
    carry = init
    for k in range(n_chunks):
        carry = step(k, carry)
    hs, _ = carry
    hfin_ref[...] = jnp.concatenate(hs, axis=1)
    hst = jnp.concatenate([hst_ref[j] for j in range(SLAB_GROUPS)], axis=1).astype(BF16)
    for t in range(0, S5_T, 2):
        cols = pl.ds(t * LANES, 2 * LANES)
        k_used = (t + 2) * LANES
        y = _dot(xcat[:, :k_used], wt_ref[pl.ds(0, k_used), cols]) + _dot(hst, wout_ref[:, cols])
        y_ref[pl.ds(t, rows, stride=S5_T), :] = y[:, :LANES]
        y_ref[pl.ds(t + 1, rows, stride=S5_T), :] = y[:, LANES:]


def _s5_batch_tile(nb, n_chunks):
    bt = nb
    while bt * n_chunks > S5_ROWS and bt % 2 == 0:
        bt //= 2
    return bt


def _s5_core(u, nb, h0_re, h0_im, weights):
    w_toep, w_in, w_out, coef = weights
    n = u.shape[0]
    length = n // nb
    n_chunks = length // S5_T
    bt = _s5_batch_tile(nb, n_chunks)
    n_bt = nb // bt
    rows = bt * n_chunks
    wide = S5_T * LANES
    sw = SLAB_GROUPS * 2 * P_STATE

    def slab_state(re, im):
        v = jnp.concatenate([re, im], axis=-1).astype(F32)
        return v.reshape(n_bt, bt, N_SLABS, sw).transpose(2, 0, 1, 3)

    h0 = jnp.stack([slab_state(h0_re, h0_im), slab_state(h0_im, h0_re)], axis=2)

    def per_slab(shape):
        return pl.BlockSpec((None,) + shape, lambda s, i: (s,) + (0,) * len(shape), pipeline_mode=pl.Buffered(1))

    tokens = pl.BlockSpec((bt * length, LANES), lambda s, i: (i, s))
    y, hfin = pl.pallas_call(
        functools.partial(_s5_body, rows=rows, n_chunks=n_chunks, bt=bt),
        grid=(N_SLABS, n_bt),
        in_specs=[tokens, per_slab((S5_T, LANES, LANES)), per_slab((S5_T, LANES, 2 * P_STATE)),
                  per_slab((S5_T, 2 * P_STATE, LANES)), per_slab((8, sw)),
                  pl.BlockSpec((None, None, 2, bt, sw), lambda s, i: (s, i, 0, 0, 0))],
        out_specs=[tokens, pl.BlockSpec((None, None, bt, sw), lambda s, i: (s, i, 0, 0))],
        out_shape=[jax.ShapeDtypeStruct((n, D_MODEL), F32), jax.ShapeDtypeStruct((N_SLABS, n_bt, bt, sw), F32)],
        scratch_shapes=[pltpu.VMEM((wide, wide), BF16), pltpu.VMEM((wide, sw), BF16), pltpu.VMEM((sw, wide), BF16),
                        pltpu.VMEM((2 * SLAB_GROUPS, rows, LANES), F32), pltpu.VMEM((SLAB_GROUPS, rows, LANES), F32)],
        compiler_params=_params(2),
        name="s5_core",
    )(u, w_toep, w_in, w_out, coef, h0)
    hfin = hfin.transpose(1, 2, 0, 3).reshape(nb, N_GROUPS, 2 * P_STATE)
    return y, hfin[..., :P_STATE], hfin[..., P_STATE:]


def _rope_tables(pos, rows):
    half = HEAD_DIM // 2
    inv = ROPE_THETA ** (-jnp.arange(half, dtype=F32) / half)
    ang = pos.astype(F32)[:, None] * inv[None, :]
    cos = jnp.cos(ang)
    sin = jnp.sin(ang)
    cos_t = jnp.concatenate([cos, cos, cos, cos], axis=-1)
    sin_t = jnp.concatenate([-sin, sin, -sin, sin], axis=-1)
    reps = rows // pos.shape[0]
    return jnp.tile(cos_t, (reps, 1)), jnp.tile(sin_t, (reps, 1))


def _rope_slab(v, cos, sin, first_half):
    partner = jnp.where(first_half, pltpu.roll(v, LANES - HEAD_DIM // 2, 1), pltpu.roll(v, HEAD_DIM // 2, 1))
    return v * cos + partner * sin


def _first_half_mask(rows):
    lane = lax.broadcasted_iota(jnp.int32, (rows, LANES), 1)
    return (lane % HEAD_DIM) < (HEAD_DIM // 2)


def _kv_rope_body(x_ref, wk_ref, wv_ref, cos_ref, sin_ref, kt_ref, ktb_ref, v_ref, vb_ref):
    xb = x_ref[...].astype(BF16)
    tm = xb.shape[0]
    half = HEAD_DIM // 2
    cos = cos_ref[...]
    sin = sin_ref[...]
    kt = _dot_nt(wk_ref[...], xb)
    for h in range(N_HEADS):
        for mp in range(2):
            lo = (2 * h + mp) * HEAD_DIM
            x1 = kt[lo:lo + half]
            x2 = kt[lo + half:lo + HEAD_DIM]
            r1 = x1 * cos - x2 * sin
            r2 = x1 * sin + x2 * cos
            kt_ref[h, mp, pl.ds(0, half), :] = r1
            kt_ref[h, mp, pl.ds(half, half), :] = r2
            ktb_ref[h, pl.ds(mp * HEAD_DIM, half), :] = r1.astype(BF16)
            ktb_ref[h, pl.ds(mp * HEAD_DIM + half, half), :] = r2.astype(BF16)
    for h in range(N_HEADS):
        cols = pl.ds(h * V_DIM, V_DIM)
        v = _dot(xb, wv_ref[:, cols])
        v_ref[pl.ds(h, tm, stride=N_HEADS), :] = v
        vb_ref[:, cols] = v.astype(BF16)


def _kv_rope(x, nb, w_kv, pos):
    n = x.shape[0]
    length = n // nb
    tm = min(ROW_TILE, length)
    tiles = length // tm
    half = HEAD_DIM // 2
    inv = ROPE_THETA ** (-jnp.arange(half, dtype=F32) / half)
    ang = inv[:, None] * pos.astype(F32)[None, :]
    wk_t = w_kv[:, :D_MODEL].T
    w_v = w_kv[:, D_MODEL:]
    table = pl.BlockSpec((half, tm), lambda i: (0, i % tiles))
    return pl.pallas_call(
        _kv_rope_body,
        grid=(n // tm,),
        in_specs=[pl.BlockSpec((tm, D_MODEL), lambda i: (i, 0)), _const_spec((D_MODEL, D_MODEL)),
                  _const_spec((D_MODEL, D_MODEL)), table, table],
        out_specs=[pl.BlockSpec((None, N_HEADS, 2, HEAD_DIM, tm), lambda i: (i // tiles, 0, 0, 0, i % tiles)),
                   pl.BlockSpec((None, N_HEADS, V_DIM, tm), lambda i: (i // tiles, 0, 0, i % tiles)),
                   pl.BlockSpec((tm * N_HEADS, V_DIM), lambda i: (i, 0)),
                   pl.BlockSpec((tm, D_MODEL), lambda i: (i, 0))],
        out_shape=[jax.ShapeDtypeStruct((nb, N_HEADS, 2, HEAD_DIM, length), F32),
                   jax.ShapeDtypeStruct((nb, N_HEADS, V_DIM, length), BF16),
                   jax.ShapeDtypeStruct((n * N_HEADS, V_DIM), F32),
                   jax.ShapeDtypeStruct((n, D_MODEL), BF16)],
        compiler_params=_params(1),
        name="kv_rope",
    )(x, wk_t, w_v, jnp.cos(ang), jnp.sin(ang))


def _lam_value(lam_ref, layer_idx):
    lv = lam_ref[...]
    s01 = jnp.sum(lv[0:1, :] * lv[1:2, :], axis=-1, keepdims=True)
    s23 = jnp.sum(lv[2:3, :] * lv[3:4, :], axis=-1, keepdims=True)
    lam_init = 0.8 - 0.6 * math.exp(-0.3 * layer_idx)
    return jnp.exp(s01) - jnp.exp(s23) + lam_init, lam_init


def _diff_attend(q, segments, lam, lam_init, subln_g):
    lane = lax.broadcasted_iota(jnp.int32, q.shape, 1)
    zero = jnp.zeros_like(q)
    q1 = jnp.where(lane < HEAD_DIM, q, zero)
    q2 = jnp.where(lane >= HEAD_DIM, q, zero)
    maps = []
    for qm in (q1, q2):
        scores = []
        for kt, _, mask in segments:
            s = _dot(qm, kt)
            scores.append(s if mask is None else jnp.where(mask, s, NEG_INF))
        m = functools.reduce(jnp.maximum, [jnp.max(s, axis=-1, keepdims=True) for s in scores])
        es = [jnp.exp2(s - m) for s in scores]
        denom = functools.reduce(jnp.add, [jnp.sum(e, axis=-1, keepdims=True) for e in es])
        acc = functools.reduce(jnp.add, [_dot(e.astype(BF16), v) for e, (_, v, _) in zip(es, segments)])
        maps.append(acc / denom)
    out = maps[0] - lam * maps[1]
    ms = jnp.mean(out * out, axis=-1, keepdims=True)
    return out * lax.rsqrt(ms + RMS_EPS) * subln_g * (1.0 - lam_init)


def _attn_prompt_body(q_ref, kt_ref, v_ref, lam_ref, g_ref, o_ref, *, length, tq, layer_idx):
    lam, lam_init = _lam_value(lam_ref, layer_idx)
    g = g_ref[...]
    r = lax.broadcasted_iota(jnp.int32, (tq, tq), 0) // CHUNK
    c = lax.broadcasted_iota(jnp.int32, (tq, tq), 1) // CHUNK
    diag_mask = c <= r
    for i in range(length // tq):
        rows = pl.ds(i * tq, tq)
        for hh in range(ATTN_HEADS):
            cols = pl.ds(hh * V_DIM, V_DIM)
            segments = []
            if i > 0:
                segments.append((kt_ref[hh, :, pl.ds(0, i * tq)], v_ref[pl.ds(0, i * tq), cols], None))
            segments.append((kt_ref[hh, :, rows], v_ref[rows, cols], diag_mask))
            o_ref[rows, cols] = _diff_attend(q_ref[rows, cols], segments, lam, lam_init, g).astype(BF16)


def _attn_prompt(q, ktb, vb, attn_lam, subln_g, layer_idx):
    nb, length, _ = q.shape
    tq = min(Q_TILE, length)
    head = pl.BlockSpec((None, length, ATTN_HEADS * V_DIM), lambda b, h: (b, 0, h))
    head_t = pl.BlockSpec((None, ATTN_HEADS, V_DIM, length), lambda b, h: (b, h, 0, 0))
    return pl.pallas_call(
        functools.partial(_attn_prompt_body, length=length, tq=tq, layer_idx=layer_idx),
        grid=(nb, N_HEADS // ATTN_HEADS),
        in_specs=[head, head_t, head, _const_spec((4, HEAD_DIM)), _const_spec((1, V_DIM))],
        out_specs=head,
        out_shape=jax.ShapeDtypeStruct((nb, length, D_MODEL), BF16),
        compiler_params=_params(2),
        name="attn_prompt",
    )(q, ktb, vb, attn_lam, subln_g)


def _attn_sample_body(q_ref, kp_ref, vp_ref, kt_ref, v_ref, lam_ref, g_ref, o_ref, m_ref, l_ref, acc_ref, *, layer_idx, tp):
    pt = pl.program_id(1)
    n_q = q_ref.shape[0]
    zero = jnp.zeros((n_q, HEAD_DIM), BF16)

    def head_scores(h, k1, k2):
        q1 = q_ref[:, h * V_DIM:h * V_DIM + HEAD_DIM]
        q2 = q_ref[:, h * V_DIM + HEAD_DIM:(h + 1) * V_DIM]
        return _dot(jnp.concatenate([q1, zero], axis=0), k1) + _dot(jnp.concatenate([zero, q2], axis=0), k2)

    def accumulate(keys_of, values_of, first):
        s = jnp.concatenate([head_scores(h, *keys_of(h)) for h in range(N_HEADS)], axis=0)
        m_tile = jnp.max(s, axis=-1, keepdims=True)
        if first:
            m_new = m_tile
        else:
            m_old = m_ref[...]
            m_new = jnp.maximum(m_old, m_tile)
            alpha = jnp.exp2(m_old - m_new)
        e = jnp.exp2(s - m_new)
        l_tile = jnp.sum(e, axis=-1, keepdims=True)
        eb = e.astype(BF16)
        pv = jnp.concatenate([_dot(eb[2 * n_q * h:2 * n_q * (h + 1)], values_of(h)) for h in range(N_HEADS)], axis=0)
        m_ref[...] = m_new
        l_ref[...] = l_tile if first else alpha * l_ref[...] + l_tile
        acc_ref[...] = pv if first else alpha * acc_ref[...] + pv

    @pl.when(pt == 0)
    def _():
        accumulate(lambda h: (kt_ref[h, pl.ds(0, HEAD_DIM), :], kt_ref[h, pl.ds(HEAD_DIM, HEAD_DIM), :]),
                   lambda h: v_ref[:, h * V_DIM:(h + 1) * V_DIM], True)

    accumulate(lambda h: (kp_ref[h, 0].astype(BF16), kp_ref[h, 1].astype(BF16)),
               lambda h: vp_ref[pl.ds(h, tp, stride=N_HEADS), :].astype(BF16), False)

    @pl.when(pt == pl.num_programs(1) - 1)
    def _():
        lam, lam_init = _lam_value(lam_ref, layer_idx)
        g = g_ref[...]
        for h in range(N_HEADS):
            r1 = pl.ds(2 * n_q * h, n_q)
            r2 = pl.ds(2 * n_q * h + n_q, n_q)
            out = acc_ref[r1, :] / l_ref[r1, :] - acc_ref[r2, :] * (lam / l_ref[r2, :])
            ms = jnp.mean(out * out, axis=-1, keepdims=True)
            o_ref[:, h * V_DIM:(h + 1) * V_DIM] = (out * lax.rsqrt(ms + RMS_EPS) * g * (1.0 - lam_init)).astype(BF16)


def _attn_sample(q, cache_k, cache_v, ktb, vb, attn_lam, subln_g, layer_idx):
    nb, length, _ = q.shape
    past = cache_k.shape[1]
    assert past % CHUNK == 0 and length <= CHUNK
    tp = PAST_TILE if past % PAST_TILE == 0 else past
    new = pl.BlockSpec((None, length, D_MODEL), lambda b, p: (b, 0, 0))
    hm = 2 * N_HEADS
    return pl.pallas_call(
        functools.partial(_attn_sample_body, layer_idx=layer_idx, tp=tp),
        grid=(nb, past // tp),
        in_specs=[new,
                  pl.BlockSpec((None, N_HEADS, 2, HEAD_DIM, tp), lambda b, p: (b, 0, 0, 0, p)),
                  pl.BlockSpec((None, tp * N_HEADS, V_DIM), lambda b, p: (b, p, 0)),
                  pl.BlockSpec((None, N_HEADS, V_DIM, length), lambda b, p: (b, 0, 0, 0)),
                  new, _const_spec((4, HEAD_DIM)), _const_spec((1, V_DIM))],
        out_specs=new,
        out_shape=jax.ShapeDtypeStruct((nb, length, D_MODEL), BF16),
        scratch_shapes=[pltpu.VMEM((hm * length, 1), F32), pltpu.VMEM((hm * length, 1), F32),
                        pltpu.VMEM((hm * length, V_DIM), F32)],
        compiler_params=_params(2),
        name="attn_sample",
    )(q, cache_k.transpose(0, 2, 3, 4, 1), cache_v.reshape(nb, past * N_HEADS, V_DIM), ktb, vb, attn_lam, subln_g)


def _run_trunk(x, pos, h0_re, h0_im, cache_k, cache_v, w):
    nb, length, _ = x.shape
    n = nb * length
    assert length % S5_T == 0
    ln_g, ln_b = w['ln_g'], w['ln_b']

    def vec(v):
        return v.reshape(1, D_MODEL).astype(F32)

    def ffn(t, l, j, ln_idx, **fused):
        return _ffn_ln(t, w['ffn_w_in'][l, j], w['ffn_w_out'][l, j], vec(ln_g[l, ln_idx]), vec(ln_b[l, ln_idx]), **fused)

    t = x.reshape(n, D_MODEL)
    new_re, new_im = [], []
    kt = v = None
    for l in range(DEPTH):
        if l == N_A:
            kt, ktb, v, vb = _kv_rope(t, nb, w['attn_w_kv'], pos)
        if l < N_A:
            t = ffn(t, l, 0, 0)
            ys, hr, hi = _s5_core(t, nb, h0_re[l], h0_im[l], w['ssm'][l])
            new_re.append(hr)
            new_im.append(hi)
            mixer = ("glu", ys, vec(w['ssm_d'][l]), w['ssm_w_glu'][l], vec(ln_g[l, 1]), vec(ln_b[l, 1]))
        else:
            j = l - N_A
            t, q = ffn(t, l, 0, 0, post=("q", w['attn_w_q'][j], pos, length))
            q = q.reshape(nb, length, D_MODEL)
            vb3 = vb.reshape(nb, length, D_MODEL)
            lam_p = w['attn_lam'][j].astype(F32)
            sub_g = w['attn_subln_g'][j].reshape(1, V_DIM).astype(F32)
            if cache_k is None:
                o = _attn_prompt(q, ktb, vb3, lam_p, sub_g, l)
            else:
                o = _attn_sample(q, cache_k, cache_v, ktb, vb3, lam_p, sub_g, l)
            mixer = ("proj", o.reshape(n, D_MODEL), w['attn_w_o'][j], vec(ln_g[l, 1]), vec(ln_b[l, 1]))
        t = ffn(t, l, 1, 2, pre=mixer)
    y = t.reshape(nb, length, D_MODEL)
    p_k = kt.transpose(0, 4, 1, 2, 3)
    p_v = v.reshape(nb, length, N_HEADS, V_DIM)
    return y, jnp.stack(new_re), jnp.stack(new_im), p_k, p_v


def kernel(x_prompt, x_sample, cache_k, cache_v, state_ssm_re, state_ssm_im,
           ln_g, ln_b, ffn_w_in, ffn_w_out,
           ssm_a_re, ssm_a_im, ssm_log_dt, ssm_b_re, ssm_b_im, ssm_c_re, ssm_c_im, ssm_d, ssm_w_glu,
           attn_w_q, attn_w_kv, attn_lam, attn_subln_g, attn_w_o):
    w = {'ln_g': ln_g, 'ln_b': ln_b,
         'ffn_w_in': ffn_w_in.astype(BF16), 'ffn_w_out': ffn_w_out.astype(BF16),
         'ssm': [_s5_weights(ssm_a_re[l], ssm_a_im[l], ssm_log_dt[l], ssm_b_re[l], ssm_b_im[l],
                             ssm_c_re[l], ssm_c_im[l]) for l in range(N_A)],
         'ssm_d': ssm_d, 'ssm_w_glu': ssm_w_glu.astype(BF16),
         'attn_w_q': attn_w_q.astype(BF16), 'attn_w_kv': attn_w_kv.astype(BF16), 'attn_lam': attn_lam,
         'attn_subln_g': attn_subln_g, 'attn_w_o': attn_w_o.astype(BF16)}
    b_p, l_p, _ = x_prompt.shape
    pos_p = jnp.arange(l_p, dtype=jnp.int32)
    h0 = jnp.zeros((N_A, b_p, N_GROUPS, P_STATE), F32)
    y_p, p_re, p_im, p_k, p_v = _run_trunk(x_prompt, pos_p, h0, h0, None, None, w)
    past = cache_k.shape[1]
    pos_s = past + jnp.arange(x_sample.shape[1], dtype=jnp.int32)
    y_s, s_re, s_im, s_k, s_v = _run_trunk(x_sample, pos_s, state_ssm_re, state_ssm_im, cache_k, cache_v, w)
    return (y_p, y_s, p_re, p_im, p_k, p_v, s_re, s_im, s_k, s_v)
```

```python
import functools
import math

import jax
import jax.numpy as jnp
from jax import lax
from jax.experimental import pallas as pl
from jax.experimental.pallas import tpu as pltpu

F32 = jnp.float32
BF16 = jnp.bfloat16

D_MODEL = 1024
DEPTH = 2
CHUNK = 64
N_A = DEPTH // 2
GROUP = 16
N_GROUPS = D_MODEL // GROUP
P_STATE = 64
HEAD_DIM = 64
V_DIM = 2 * HEAD_DIM
N_HEADS = D_MODEL // V_DIM
D_FF = ((8 * D_MODEL // 3 + 127) // 128) * 128
ROPE_THETA = 10000.0
ALPHA = (2 * DEPTH) ** 0.25
LN_EPS = 1e-5
RMS_EPS = 1e-5
NEG_INF = -1e30
LOG2_E = math.log2(math.e)

LANES = 128
ROW_TILE = 512
FF_TILE = 256
S5_T = 16
SLAB_GROUPS = LANES // GROUP
N_SLABS = N_GROUPS // SLAB_GROUPS
S5_ROWS = 512
Q_TILE = 512
ATTN_HEADS = 2
PAST_TILE = 1024
VMEM_LIMIT = 56 * 1024 * 1024


def _params(n_axes):
    return pltpu.CompilerParams(dimension_semantics=("arbitrary",) * n_axes, vmem_limit_bytes=VMEM_LIMIT)


def _row_tile(n):
    return ROW_TILE if n % ROW_TILE == 0 else n


def _const_spec(shape):
    return pl.BlockSpec(shape, lambda *_: (0,) * len(shape), pipeline_mode=pl.Buffered(1))


def _layer_norm(v, g, b):
    mu = jnp.mean(v, axis=-1, keepdims=True)
    c = v - mu
    var = jnp.mean(c * c, axis=-1, keepdims=True)
    return c * lax.rsqrt(var + LN_EPS) * g + b


def _sigmoid(v):
    return 0.5 * jnp.tanh(0.5 * v) + 0.5


def _gelu_tanh(v):
    return 0.5 * v * (1.0 + jnp.tanh(math.sqrt(2.0 / math.pi) * (v + 0.044715 * (v * v * v))))


def _dot(a, b):
    return jnp.dot(a, b, preferred_element_type=F32)


def _dot_nt(a, b):
    return lax.dot_general(a, b, (((1,), (1,)), ((), ())), preferred_element_type=F32)


def _swiglu_ln(x, win_ref, wout_ref, g, b):
    xb = x.astype(BF16)
    acc = None
    for c in range(D_FF // FF_TILE):
        a = _dot(xb, win_ref[:, pl.ds(c * FF_TILE, FF_TILE)])
        gate = _dot(xb, win_ref[:, pl.ds(D_FF + c * FF_TILE, FF_TILE)])
        h = (a * _sigmoid(a) * gate).astype(BF16)
        part = _dot(h, wout_ref[pl.ds(c * FF_TILE, FF_TILE), :])
        acc = part if acc is None else acc + part
    return _layer_norm(ALPHA * x + 0.5 * acc, g, b)


def _glu_ln(x, ys, d_skip, wg_ref, g, b):
    z = _gelu_tanh(ys + d_skip * x).astype(BF16)
    out = _dot(z, wg_ref[:, pl.ds(0, D_MODEL)])
    gate = _dot(z, wg_ref[:, pl.ds(D_MODEL, D_MODEL)])
    return _layer_norm(ALPHA * x + out * _sigmoid(gate), g, b)


def _q_rope(x, wq_ref, cos, sin, q_ref):
    xb = x.astype(BF16)
    first_half = _first_half_mask(xb.shape[0])
    scale = HEAD_DIM ** -0.5 * LOG2_E
    q_all = _dot(xb, wq_ref[...])
    for j in range(D_MODEL // LANES):
        q = _rope_slab(q_all[:, j * LANES:(j + 1) * LANES], cos, sin, first_half)
        q_ref[:, pl.ds(j * LANES, LANES)] = (q * scale).astype(BF16)


def _ffn_body(*refs, pre, post):
    refs = list(refs)
    x = refs.pop(0)[...]
    if pre == "glu":
        ys_ref, d_ref, wg_ref, g0_ref, b0_ref = refs[:5]
        del refs[:5]
        x = _glu_ln(x, ys_ref[...], d_ref[...], wg_ref, g0_ref[...], b0_ref[...])
    elif pre == "proj":
        o_ref, wo_ref, g0_ref, b0_ref = refs[:4]
        del refs[:4]
        x = _layer_norm(ALPHA * x + _dot(o_ref[...], wo_ref[...]), g0_ref[...], b0_ref[...])
    win_ref, wout_ref, g_ref, b_ref = refs[:4]
    del refs[:4]
    y = _swiglu_ln(x, win_ref, wout_ref, g_ref[...], b_ref[...])
    if post == "q":
        wq_ref, cos_ref, sin_ref, y_ref, q_ref = refs
        _q_rope(y, wq_ref, cos_ref[...], sin_ref[...], q_ref)
    else:
        (y_ref,) = refs
    y_ref[...] = y


def _ffn_ln(x, w_in, w_out, g, b, pre=None, post=None):
    n = x.shape[0]
    tm = _row_tile(n)
    row = pl.BlockSpec((tm, D_MODEL), lambda i: (i, 0))
    vec = _const_spec((1, D_MODEL))
    args, specs = [x], [row]
    if pre is not None and pre[0] == "glu":
        args += list(pre[1:])
        specs += [row, vec, _const_spec((D_MODEL, 2 * D_MODEL)), vec, vec]
    elif pre is not None:
        args += list(pre[1:])
        specs += [row, _const_spec((D_MODEL, D_MODEL)), vec, vec]
    args += [w_in, w_out, g, b]
    specs += [_const_spec((D_MODEL, 2 * D_FF)), _const_spec((D_FF, D_MODEL)), vec, vec]
    out_specs, out_shape = [row], [jax.ShapeDtypeStruct((n, D_MODEL), F32)]
    if post is not None:
        _, w_q, pos, length = post
        table_rows = max(length, tm)
        assert table_rows % length == 0 and table_rows % tm == 0
        cos_t, sin_t = _rope_tables(pos, table_rows)
        n_table_tiles = table_rows // tm
        table = pl.BlockSpec((tm, LANES), lambda i: (i % n_table_tiles, 0))
        args += [w_q, cos_t, sin_t]
        specs += [_const_spec((D_MODEL, D_MODEL)), table, table]
        out_specs.append(row)
        out_shape.append(jax.ShapeDtypeStruct((n, D_MODEL), BF16))
    outs = pl.pallas_call(
        functools.partial(_ffn_body, pre=None if pre is None else pre[0], post=None if post is None else post[0]),
        grid=(n // tm,),
        in_specs=specs,
        out_specs=out_specs,
        out_shape=out_shape,
        compiler_params=_params(1),
        name="ffn_ln",
    )(*args)
    return outs if post is not None else outs[0]


def _s5_weights(a_re, a_im, log_dt, b_re, b_im, c_re, c_im):
    hi = lax.Precision.HIGHEST
    lam_re = jnp.minimum(a_re.astype(F32), -1e-4)
    lam_im = a_im.astype(F32)
    dt = jnp.exp(log_dt.astype(F32))[:, None]
    mag = jnp.exp(lam_re * dt)
    lb_re = mag * jnp.cos(lam_im * dt)
    lb_im = mag * jnp.sin(lam_im * dt)
    den = lam_re * lam_re + lam_im * lam_im
    n_re = lb_re - 1.0
    k_re = (n_re * lam_re + lb_im * lam_im) / den
    k_im = (lb_im * lam_re - n_re * lam_im) / den
    b_re = b_re.astype(F32)
    b_im = b_im.astype(F32)
    bb_re = k_re[..., None] * b_re - k_im[..., None] * b_im
    bb_im = k_re[..., None] * b_im + k_im[..., None] * b_re
    c_re = c_re.astype(F32)
    c_im = c_im.astype(F32)

    def power_step(carry, _):
        pr, pi = carry
        return (pr * lb_re - pi * lb_im, pr * lb_im + pi * lb_re), (pr, pi)

    _, (pw_re, pw_im) = lax.scan(power_step, (jnp.ones_like(lb_re), jnp.zeros_like(lb_re)), None,
                                 length=S5_T + 1)

    cl_re = c_re[None] * pw_re[:, :, None, :] - c_im[None] * pw_im[:, :, None, :]
    cl_im = c_re[None] * pw_im[:, :, None, :] + c_im[None] * pw_re[:, :, None, :]

    taps = (jnp.einsum('ngdp,gpc->ngcd', cl_re[:S5_T], bb_re, precision=hi)
            - jnp.einsum('ngdp,gpc->ngcd', cl_im[:S5_T], bb_im, precision=hi))
    state_w = SLAB_GROUPS * 2 * P_STATE

    lane_of = jnp.arange(LANES)[None, :] % GROUP
    repeat_lanes = (lane_of == jnp.arange(GROUP)[:, None]).astype(F32)
    w_toep = jnp.einsum('ngcd,dl->ngcl', taps, repeat_lanes, precision=hi)
    w_toep = w_toep.reshape(S5_T, N_SLABS, LANES, LANES).transpose(1, 0, 2, 3).astype(BF16)

    rev_re = pw_re[S5_T - 1::-1][:S5_T]
    rev_im = pw_im[S5_T - 1::-1][:S5_T]
    in_re = rev_re[..., None] * bb_re[None] - rev_im[..., None] * bb_im[None]
    in_im = rev_re[..., None] * bb_im[None] + rev_im[..., None] * bb_re[None]
    w_in = jnp.concatenate([in_re, in_im], axis=2)
    w_in = jnp.einsum('ck,sgpk->csgp', jnp.eye(GROUP, dtype=F32), w_in, precision=hi).transpose(1, 2, 0, 3)
    w_in = w_in.reshape(S5_T, N_SLABS, LANES, 2 * P_STATE).transpose(1, 0, 2, 3).astype(BF16)

    w_out = jnp.concatenate([cl_re[1:], -cl_im[1:]], axis=3)
    w_out = w_out.reshape(S5_T, N_SLABS, SLAB_GROUPS, GROUP, 2 * P_STATE).transpose(1, 0, 4, 2, 3)
    w_out = w_out.reshape(N_SLABS, S5_T, 2 * P_STATE, LANES).astype(BF16)

    lt_re, lt_im = pw_re[S5_T], pw_im[S5_T]
    coef = jnp.stack([jnp.concatenate([lt_re, lt_re], -1),
                      jnp.concatenate([-lt_im, lt_im], -1),
                      jnp.concatenate([lt_im, -lt_im], -1)], axis=0)
    coef = coef.reshape(3, N_SLABS, state_w).transpose(1, 0, 2)
    coef = jnp.concatenate([coef, jnp.zeros((N_SLABS, 5, state_w), F32)], axis=1)
    return w_toep, w_in, w_out, coef


def _s5_expand_weights(toep_ref, cin_ref, cout_ref, wt_ref, win_ref, wout_ref):
    group_of_row = lax.broadcasted_iota(jnp.int32, (LANES, LANES), 0) // GROUP
    group_of_col = lax.broadcasted_iota(jnp.int32, (LANES, LANES), 1) // GROUP
    zero = jnp.zeros((LANES, LANES), BF16)
    taps = [jnp.where(group_of_row == group_of_col, toep_ref[n], zero) for n in range(S5_T)]
    for s in range(S5_T):
        rows = pl.ds(s * LANES, LANES)
        for t in range(S5_T):
            wt_ref[rows, pl.ds(t * LANES, LANES)] = taps[t - s] if t >= s else zero
        tile_in = cin_ref[s]
        tile_out = cout_ref[s]
        for j in range(SLAB_GROUPS):
            win_ref[rows, pl.ds(j * LANES, LANES)] = jnp.where(group_of_row == j, tile_in, zero)
            wout_ref[pl.ds(j * LANES, LANES), rows] = jnp.where(group_of_col == j, tile_out, zero)


def _s5_body(x_ref, toep_ref, cin_ref, cout_ref, coef_ref, h0_ref, y_ref, hfin_ref,
             wt_ref, win_ref, wout_ref, s_ref, hst_ref, *, rows, n_chunks, bt):
    @pl.when(pl.program_id(1) == 0)
    def _():
        _s5_expand_weights(toep_ref, cin_ref, cout_ref, wt_ref, win_ref, wout_ref)

    xcat = jnp.concatenate([x_ref[pl.ds(s, rows, stride=S5_T), :].astype(BF16) for s in range(S5_T)], axis=1)
    s_all = _dot(xcat, win_ref[...])
    for j in range(SLAB_GROUPS):
        sj = s_all[:, j * LANES:(j + 1) * LANES]
        s_ref[j] = sj
        s_ref[SLAB_GROUPS + j] = pltpu.roll(sj, P_STATE, 1)
    coef = coef_ref[...]
    h0 = h0_ref[...]

    def lanes(v, j):
        return v[:, j * LANES:(j + 1) * LANES]

    def step(k, carry):
        hs, hts = carry
        idx = pl.ds(k, bt, stride=n_chunks)
        new_h, new_ht = [], []
        for j in range(SLAB_GROUPS):
            hst_ref[j, idx, :] = hs[j]
            c_self, c_swap, c_swap_t = lanes(coef[0:1], j), lanes(coef[1:2], j), lanes(coef[2:3], j)
            new_h.append(c_self * hs[j] + c_swap * hts[j] + s_ref[j, idx, :])
            new_ht.append(c_self * hts[j] + c_swap_t * hs[j] + s_ref[SLAB_GROUPS + j, idx, :])
        return tuple(new_h), tuple(new_ht)

    init = (tuple(lanes(h0[0], j) for j in range(SLAB_GROUPS)), tuple(lanes(h0[1], j) for j in range(SLAB_GROUPS)))
    carry = init
    for k in range(n_chunks):
        carry = step(k, carry)
    hs, _ = carry
    hfin_ref[...] = jnp.concatenate(hs, axis=1)
    hst = jnp.concatenate([hst_ref[j] for j in range(SLAB_GROUPS)], axis=1).astype(BF16)
    for t in range(0, S5_T, 2):
        cols = pl.ds(t * LANES, 2 * LANES)
        k_used = (t + 2) * LANES
        y = _dot(xcat[:, :k_used], wt_ref[pl.ds(0, k_used), cols]) + _dot(hst, wout_ref[:, cols])
        y_ref[pl.ds(t, rows, stride=S5_T), :] = y[:, :LANES]
        y_ref[pl.ds(t + 1, rows, stride=S5_T), :] = y[:, LANES:]


def _s5_batch_tile(nb, n_chunks):
    bt = nb
    while bt * n_chunks > S5_ROWS and bt % 2 == 0:
        bt //= 2
    return bt


def _s5_core(u, nb, h0_re, h0_im, weights):
    w_toep, w_in, w_out, coef = weights
    n = u.shape[0]
    length = n // nb
    n_chunks = length // S5_T
    bt = _s5_batch_tile(nb, n_chunks)
    n_bt = nb // bt
    rows = bt * n_chunks
    wide = S5_T * LANES
    sw = SLAB_GROUPS * 2 * P_STATE

    def slab_state(re, im):
        v = jnp.concatenate([re, im], axis=-1).astype(F32)
        return v.reshape(n_bt, bt, N_SLABS, sw).transpose(2, 0, 1, 3)

    h0 = jnp.stack([slab_state(h0_re, h0_im), slab_state(h0_im, h0_re)], axis=2)

    def per_slab(shape):
        return pl.BlockSpec((None,) + shape, lambda s, i: (s,) + (0,) * len(shape), pipeline_mode=pl.Buffered(1))

    tokens = pl.BlockSpec((bt * length, LANES), lambda s, i: (i, s))
    y, hfin = pl.pallas_call(
        functools.partial(_s5_body, rows=rows, n_chunks=n_chunks, bt=bt),
        grid=(N_SLABS, n_bt),
        in_specs=[tokens, per_slab((S5_T, LANES, LANES)), per_slab((S5_T, LANES, 2 * P_STATE)),
                  per_slab((S5_T, 2 * P_STATE, LANES)), per_slab((8, sw)),
                  pl.BlockSpec((None, None, 2, bt, sw), lambda s, i: (s, i, 0, 0, 0))],
        out_specs=[tokens, pl.BlockSpec((None, None, bt, sw), lambda s, i: (s, i, 0, 0))],
        out_shape=[jax.ShapeDtypeStruct((n, D_MODEL), F32), jax.ShapeDtypeStruct((N_SLABS, n_bt, bt, sw), F32)],
        scratch_shapes=[pltpu.VMEM((wide, wide), BF16), pltpu.VMEM((wide, sw), BF16), pltpu.VMEM((sw, wide), BF16),
                        pltpu.VMEM((2 * SLAB_GROUPS, rows, LANES), F32), pltpu.VMEM((SLAB_GROUPS, rows, LANES), F32)],
        compiler_params=_params(2),
        name="s5_core",
    )(u, w_toep, w_in, w_out, coef, h0)
    hfin = hfin.transpose(1, 2, 0, 3).reshape(nb, N_GROUPS, 2 * P_STATE)
    return y, hfin[..., :P_STATE], hfin[..., P_STATE:]


def _rope_tables(pos, rows):
    half = HEAD_DIM // 2
    inv = ROPE_THETA ** (-jnp.arange(half, dtype=F32) / half)
    ang = pos.astype(F32)[:, None] * inv[None, :]
    cos = jnp.cos(ang)
    sin = jnp.sin(ang)
    cos_t = jnp.concatenate([cos, cos, cos, cos], axis=-1)
    sin_t = jnp.concatenate([-sin, sin, -sin, sin], axis=-1)
    reps = rows // pos.shape[0]
    return jnp.tile(cos_t, (reps, 1)), jnp.tile(sin_t, (reps, 1))


def _rope_slab(v, cos, sin, first_half):
    partner = jnp.where(first_half, pltpu.roll(v, LANES - HEAD_DIM // 2, 1), pltpu.roll(v, HEAD_DIM // 2, 1))
    return v * cos + partner * sin


def _first_half_mask(rows):
    lane = lax.broadcasted_iota(jnp.int32, (rows, LANES), 1)
    return (lane % HEAD_DIM) < (HEAD_DIM // 2)


def _kv_rope_body(x_ref, wk_ref, wv_ref, cos_ref, sin_ref, kt_ref, ktb_ref, v_ref, vb_ref):
    xb = x_ref[...].astype(BF16)
    tm = xb.shape[0]
    half = HEAD_DIM // 2
    cos = cos_ref[...]
    sin = sin_ref[...]
    kt = _dot_nt(wk_ref[...], xb)
    for h in range(N_HEADS):
        for mp in range(2):
            lo = (2 * h + mp) * HEAD_DIM
            x1 = kt[lo:lo + half]
            x2 = kt[lo + half:lo + HEAD_DIM]
            r1 = x1 * cos - x2 * sin
            r2 = x1 * sin + x2 * cos
            kt_ref[h, mp, pl.ds(0, half), :] = r1
            kt_ref[h, mp, pl.ds(half, half), :] = r2
            ktb_ref[h, pl.ds(mp * HEAD_DIM, half), :] = r1.astype(BF16)
            ktb_ref[h, pl.ds(mp * HEAD_DIM + half, half), :] = r2.astype(BF16)
    v_all = _dot(xb, wv_ref[...])
    vb_ref[...] = v_all.astype(BF16)
    for h in range(N_HEADS):
        v_ref[pl.ds(h, tm, stride=N_HEADS), :] = v_all[:, h * V_DIM:(h + 1) * V_DIM]


def _kv_rope(x, nb, w_kv, pos):
    n = x.shape[0]
    length = n // nb
    tm = min(ROW_TILE, length)
    tiles = length // tm
    half = HEAD_DIM // 2
    inv = ROPE_THETA ** (-jnp.arange(half, dtype=F32) / half)
    ang = inv[:, None] * pos.astype(F32)[None, :]
    wk_t = w_kv[:, :D_MODEL].T
    w_v = w_kv[:, D_MODEL:]
    table = pl.BlockSpec((half, tm), lambda i: (0, i % tiles))
    return pl.pallas_call(
        _kv_rope_body,
        grid=(n // tm,),
        in_specs=[pl.BlockSpec((tm, D_MODEL), lambda i: (i, 0)), _const_spec((D_MODEL, D_MODEL)),
                  _const_spec((D_MODEL, D_MODEL)), table, table],
        out_specs=[pl.BlockSpec((None, N_HEADS, 2, HEAD_DIM, tm), lambda i: (i // tiles, 0, 0, 0, i % tiles)),
                   pl.BlockSpec((None, N_HEADS, V_DIM, tm), lambda i: (i // tiles, 0, 0, i % tiles)),
                   pl.BlockSpec((tm * N_HEADS, V_DIM), lambda i: (i, 0)),
                   pl.BlockSpec((tm, D_MODEL), lambda i: (i, 0))],
        out_shape=[jax.ShapeDtypeStruct((nb, N_HEADS, 2, HEAD_DIM, length), F32),
                   jax.ShapeDtypeStruct((nb, N_HEADS, V_DIM, length), BF16),
                   jax.ShapeDtypeStruct((n * N_HEADS, V_DIM), F32),
                   jax.ShapeDtypeStruct((n, D_MODEL), BF16)],
        compiler_params=_params(1),
        name="kv_rope",
    )(x, wk_t, w_v, jnp.cos(ang), jnp.sin(ang))


def _lam_value(lam_ref, layer_idx):
    lv = lam_ref[...]
    s01 = jnp.sum(lv[0:1, :] * lv[1:2, :], axis=-1, keepdims=True)
    s23 = jnp.sum(lv[2:3, :] * lv[3:4, :], axis=-1, keepdims=True)
    lam_init = 0.8 - 0.6 * math.exp(-0.3 * layer_idx)
    return jnp.exp(s01) - jnp.exp(s23) + lam_init, lam_init


def _diff_attend(q, segments, lam, lam_init, subln_g):
    lane = lax.broadcasted_iota(jnp.int32, q.shape, 1)
    zero = jnp.zeros_like(q)
    q1 = jnp.where(lane < HEAD_DIM, q, zero)
    q2 = jnp.where(lane >= HEAD_DIM, q, zero)
    maps = []
    for qm in (q1, q2):
        scores = []
        for kt, _, mask in segments:
            s = _dot(qm, kt)
            scores.append(s if mask is None else jnp.where(mask, s, NEG_INF))
        m = functools.reduce(jnp.maximum, [jnp.max(s, axis=-1, keepdims=True) for s in scores])
        es = [jnp.exp2(s - m) for s in scores]
        denom = functools.reduce(jnp.add, [jnp.sum(e, axis=-1, keepdims=True) for e in es])
        acc = functools.reduce(jnp.add, [_dot(e.astype(BF16), v) for e, (_, v, _) in zip(es, segments)])
        maps.append(acc / denom)
    out = maps[0] - lam * maps[1]
    ms = jnp.mean(out * out, axis=-1, keepdims=True)
    return out * lax.rsqrt(ms + RMS_EPS) * subln_g * (1.0 - lam_init)


def _attn_prompt_body(q_ref, kt_ref, v_ref, lam_ref, g_ref, o_ref, *, length, tq, layer_idx):
    lam, lam_init = _lam_value(lam_ref, layer_idx)
    g = g_ref[...]
    r = lax.broadcasted_iota(jnp.int32, (tq, tq), 0) // CHUNK
    c = lax.broadcasted_iota(jnp.int32, (tq, tq), 1) // CHUNK
    diag_mask = c <= r
    for i in range(length // tq):
        rows = pl.ds(i * tq, tq)
        for hh in range(ATTN_HEADS):
            cols = pl.ds(hh * V_DIM, V_DIM)
            segments = []
            if i > 0:
                segments.append((kt_ref[hh, :, pl.ds(0, i * tq)], v_ref[pl.ds(0, i * tq), cols], None))
            segments.append((kt_ref[hh, :, rows], v_ref[rows, cols], diag_mask))
            o_ref[rows, cols] = _diff_attend(q_ref[rows, cols], segments, lam, lam_init, g).astype(BF16)


def _attn_prompt(q, ktb, vb, attn_lam, subln_g, layer_idx):
    nb, length, _ = q.shape
    tq = min(Q_TILE, length)
    head = pl.BlockSpec((None, length, ATTN_HEADS * V_DIM), lambda b, h: (b, 0, h))
    head_t = pl.BlockSpec((None, ATTN_HEADS, V_DIM, length), lambda b, h: (b, h, 0, 0))
    return pl.pallas_call(
        functools.partial(_attn_prompt_body, length=length, tq=tq, layer_idx=layer_idx),
        grid=(nb, N_HEADS // ATTN_HEADS),
        in_specs=[head, head_t, head, _const_spec((4, HEAD_DIM)), _const_spec((1, V_DIM))],
        out_specs=head,
        out_shape=jax.ShapeDtypeStruct((nb, length, D_MODEL), BF16),
        compiler_params=_params(2),
        name="attn_prompt",
    )(q, ktb, vb, attn_lam, subln_g)


def _attn_sample_body(q_ref, kp_ref, vp_ref, kt_ref, v_ref, lam_ref, g_ref, o_ref, m_ref, l_ref, acc_ref, *, layer_idx, tp):
    pt = pl.program_id(1)
    n_q = q_ref.shape[0]
    zero = jnp.zeros((n_q, HEAD_DIM), BF16)

    def head_scores(h, k1, k2):
        q1 = q_ref[:, h * V_DIM:h * V_DIM + HEAD_DIM]
        q2 = q_ref[:, h * V_DIM + HEAD_DIM:(h + 1) * V_DIM]
        return _dot(jnp.concatenate([q1, zero], axis=0), k1) + _dot(jnp.concatenate([zero, q2], axis=0), k2)

    def accumulate(keys_of, values_of, first):
        s = jnp.concatenate([head_scores(h, *keys_of(h)) for h in range(N_HEADS)], axis=0)
        m_tile = jnp.max(s, axis=-1, keepdims=True)
        if first:
            m_new = m_tile
        else:
            m_old = m_ref[...]
            m_new = jnp.maximum(m_old, m_tile)
            alpha = jnp.exp2(m_old - m_new)
        e = jnp.exp2(s - m_new)
        l_tile = jnp.sum(e, axis=-1, keepdims=True)
        eb = e.astype(BF16)
        pv = jnp.concatenate([_dot(eb[2 * n_q * h:2 * n_q * (h + 1)], values_of(h)) for h in range(N_HEADS)], axis=0)
        m_ref[...] = m_new
        l_ref[...] = l_tile if first else alpha * l_ref[...] + l_tile
        acc_ref[...] = pv if first else alpha * acc_ref[...] + pv

    @pl.when(pt == 0)
    def _():
        accumulate(lambda h: (kt_ref[h, pl.ds(0, HEAD_DIM), :], kt_ref[h, pl.ds(HEAD_DIM, HEAD_DIM), :]),
                   lambda h: v_ref[:, h * V_DIM:(h + 1) * V_DIM], True)

    accumulate(lambda h: (kp_ref[h, 0].astype(BF16), kp_ref[h, 1].astype(BF16)),
               lambda h: vp_ref[pl.ds(h, tp, stride=N_HEADS), :].astype(BF16), False)

    @pl.when(pt == pl.num_programs(1) - 1)
    def _():
        lam, lam_init = _lam_value(lam_ref, layer_idx)
        g = g_ref[...]
        for h in range(N_HEADS):
            r1 = pl.ds(2 * n_q * h, n_q)
            r2 = pl.ds(2 * n_q * h + n_q, n_q)
            out = acc_ref[r1, :] / l_ref[r1, :] - acc_ref[r2, :] * (lam / l_ref[r2, :])
            ms = jnp.mean(out * out, axis=-1, keepdims=True)
            o_ref[:, h * V_DIM:(h + 1) * V_DIM] = (out * lax.rsqrt(ms + RMS_EPS) * g * (1.0 - lam_init)).astype(BF16)


def _attn_sample(q, cache_k, cache_v, ktb, vb, attn_lam, subln_g, layer_idx):
    nb, length, _ = q.shape
    past = cache_k.shape[1]
    assert past % CHUNK == 0 and length <= CHUNK
    tp = PAST_TILE if past % PAST_TILE == 0 else past
    new = pl.BlockSpec((None, length, D_MODEL), lambda b, p: (b, 0, 0))
    hm = 2 * N_HEADS
    return pl.pallas_call(
        functools.partial(_attn_sample_body, layer_idx=layer_idx, tp=tp),
        grid=(nb, past // tp),
        in_specs=[new,
                  pl.BlockSpec((None, N_HEADS, 2, HEAD_DIM, tp), lambda b, p: (b, 0, 0, 0, p)),
                  pl.BlockSpec((None, tp * N_HEADS, V_DIM), lambda b, p: (b, p, 0)),
                  pl.BlockSpec((None, N_HEADS, V_DIM, length), lambda b, p: (b, 0, 0, 0)),
                  new, _const_spec((4, HEAD_DIM)), _const_spec((1, V_DIM))],
        out_specs=new,
        out_shape=jax.ShapeDtypeStruct((nb, length, D_MODEL), BF16),
        scratch_shapes=[pltpu.VMEM((hm * length, 1), F32), pltpu.VMEM((hm * length, 1), F32),
                        pltpu.VMEM((hm * length, V_DIM), F32)],
        compiler_params=_params(2),
        name="attn_sample",
    )(q, cache_k.transpose(0, 2, 3, 4, 1), cache_v.reshape(nb, past * N_HEADS, V_DIM), ktb, vb, attn_lam, subln_g)


def _run_trunk(x, pos, h0_re, h0_im, cache_k, cache_v, w):
    nb, length, _ = x.shape
    n = nb * length
    assert length % S5_T == 0
    ln_g, ln_b = w['ln_g'], w['ln_b']

    def vec(v):
        return v.reshape(1, D_MODEL).astype(F32)

    def ffn(t, l, j, ln_idx, **fused):
        return _ffn_ln(t, w['ffn_w_in'][l, j], w['ffn_w_out'][l, j], vec(ln_g[l, ln_idx]), vec(ln_b[l, ln_idx]), **fused)

    t = x.reshape(n, D_MODEL)
    new_re, new_im = [], []
    kt = v = None
    for l in range(DEPTH):
        if l == N_A:
            kt, ktb, v, vb = _kv_rope(t, nb, w['attn_w_kv'], pos)
        if l < N_A:
            t = ffn(t, l, 0, 0)
            ys, hr, hi = _s5_core(t, nb, h0_re[l], h0_im[l], w['ssm'][l])
            new_re.append(hr)
            new_im.append(hi)
            mixer = ("glu", ys, vec(w['ssm_d'][l]), w['ssm_w_glu'][l], vec(ln_g[l, 1]), vec(ln_b[l, 1]))
        else:
            j = l - N_A
            t, q = ffn(t, l, 0, 0, post=("q", w['attn_w_q'][j], pos, length))
            q = q.reshape(nb, length, D_MODEL)
            vb3 = vb.reshape(nb, length, D_MODEL)
            lam_p = w['attn_lam'][j].astype(F32)
            sub_g = w['attn_subln_g'][j].reshape(1, V_DIM).astype(F32)
            if cache_k is None:
                o = _attn_prompt(q, ktb, vb3, lam_p, sub_g, l)
            else:
                o = _attn_sample(q, cache_k, cache_v, ktb, vb3, lam_p, sub_g, l)
            mixer = ("proj", o.reshape(n, D_MODEL), w['attn_w_o'][j], vec(ln_g[l, 1]), vec(ln_b[l, 1]))
        t = ffn(t, l, 1, 2, pre=mixer)
    y = t.reshape(nb, length, D_MODEL)
    p_k = kt.transpose(0, 4, 1, 2, 3)
    p_v = v.reshape(nb, length, N_HEADS, V_DIM)
    return y, jnp.stack(new_re), jnp.stack(new_im), p_k, p_v


def kernel(x_prompt, x_sample, cache_k, cache_v, state_ssm_re, state_ssm_im,
           ln_g, ln_b, ffn_w_in, ffn_w_out,
           ssm_a_re, ssm_a_im, ssm_log_dt, ssm_b_re, ssm_b_im, ssm_c_re, ssm_c_im, ssm_d, ssm_w_glu,
           attn_w_q, attn_w_kv, attn_lam, attn_subln_g, attn_w_o):
    w = {'ln_g': ln_g, 'ln_b': ln_b,
         'ffn_w_in': ffn_w_in.astype(BF16), 'ffn_w_out': ffn_w_out.astype(BF16),
         'ssm': [_s5_weights(ssm_a_re[l], ssm_a_im[l], ssm_log_dt[l], ssm_b_re[l], ssm_b_im[l],
                             ssm_c_re[l], ssm_c_im[l]) for l in range(N_A)],
         'ssm_d': ssm_d, 'ssm_w_glu': ssm_w_glu.astype(BF16),
         'attn_w_q': attn_w_q.astype(BF16), 'attn_w_kv': attn_w_kv.astype(BF16), 'attn_lam': attn_lam,
         'attn_subln_g': attn_subln_g, 'attn_w_o': attn_w_o.astype(BF16)}
    b_p, l_p, _ = x_prompt.shape
    pos_p = jnp.arange(l_p, dtype=jnp.int32)
    h0 = jnp.zeros((N_A, b_p, N_GROUPS, P_STATE), F32)
    y_p, p_re, p_im, p_k, p_v = _run_trunk(x_prompt, pos_p, h0, h0, None, None, w)
    past = cache_k.shape[1]
    pos_s = past + jnp.arange(x_sample.shape[1], dtype=jnp.int32)
    y_s, s_re, s_im, s_k, s_v = _run_trunk(x_sample, pos_s, state_ssm_re, state_ssm_im, cache_k, cache_v, w)
    return (y_p, y_s, p_re, p_im, p_k, p_v, s_re, s_im, s_k, s_v)
```

```python
import functools
import math

import jax
import jax.numpy as jnp
from jax import lax
from jax.experimental import pallas as pl
from jax.experimental.pallas import tpu as pltpu

F32 = jnp.float32
BF16 = jnp.bfloat16

D_MODEL = 1024
DEPTH = 2
CHUNK = 64
N_A = DEPTH // 2
GROUP = 16
N_GROUPS = D_MODEL // GROUP
P_STATE = 64
HEAD_DIM = 64
V_DIM = 2 * HEAD_DIM
N_HEADS = D_MODEL // V_DIM
D_FF = ((8 * D_MODEL // 3 + 127) // 128) * 128
ROPE_THETA = 10000.0
ALPHA = (2 * DEPTH) ** 0.25
LN_EPS = 1e-5
RMS_EPS = 1e-5
NEG_INF = -1e30
LOG2_E = math.log2(math.e)

LANES = 128
ROW_TILE = 512
FF_TILE = 256
S5_T = 16
SLAB_GROUPS = LANES // GROUP
N_SLABS = N_GROUPS // SLAB_GROUPS
S5_ROWS = 512
Q_TILE = 512
ATTN_HEADS = 2
PAST_TILE = 1024
VMEM_LIMIT = 56 * 1024 * 1024


def _params(n_axes):
    return pltpu.CompilerParams(dimension_semantics=("arbitrary",) * n_axes, vmem_limit_bytes=VMEM_LIMIT)


def _row_tile(n):
    return ROW_TILE if n % ROW_TILE == 0 else n


def _const_spec(shape):
    return pl.BlockSpec(shape, lambda *_: (0,) * len(shape), pipeline_mode=pl.Buffered(1))


def _layer_norm(v, g, b):
    mu = jnp.mean(v, axis=-1, keepdims=True)
    c = v - mu
    var = jnp.mean(c * c, axis=-1, keepdims=True)
    return c * lax.rsqrt(var + LN_EPS) * g + b


def _sigmoid(v):
    return 0.5 * jnp.tanh(0.5 * v) + 0.5


def _gelu_tanh(v):
    return 0.5 * v * (1.0 + jnp.tanh(math.sqrt(2.0 / math.pi) * (v + 0.044715 * (v * v * v))))


def _dot(a, b):
    return jnp.dot(a, b, preferred_element_type=F32)


def _dot_nt(a, b):
    return lax.dot_general(a, b, (((1,), (1,)), ((), ())), preferred_element_type=F32)


def _swiglu_ln(x, win_ref, wout_ref, g, b):
    xb = x.astype(BF16)
    acc = None
    for c in range(D_FF // FF_TILE):
        a = _dot(xb, win_ref[:, pl.ds(c * FF_TILE, FF_TILE)])
        gate = _dot(xb, win_ref[:, pl.ds(D_FF + c * FF_TILE, FF_TILE)])
        h = (a * _sigmoid(a) * gate).astype(BF16)
        part = _dot(h, wout_ref[pl.ds(c * FF_TILE, FF_TILE), :])
        acc = part if acc is None else acc + part
    return _layer_norm(ALPHA * x + 0.5 * acc, g, b)


def _glu_ln(x, ys, d_skip, wg_ref, g, b):
    z = _gelu_tanh(ys + d_skip * x).astype(BF16)
    out = _dot(z, wg_ref[:, pl.ds(0, D_MODEL)])
    gate = _dot(z, wg_ref[:, pl.ds(D_MODEL, D_MODEL)])
    return _layer_norm(ALPHA * x + out * _sigmoid(gate), g, b)


def _q_rope(x, wq_ref, cos, sin, q_ref):
    xb = x.astype(BF16)
    first_half = _first_half_mask(xb.shape[0])
    scale = HEAD_DIM ** -0.5 * LOG2_E
    q_all = _dot(xb, wq_ref[...])
    for j in range(D_MODEL // LANES):
        q = _rope_slab(q_all[:, j * LANES:(j + 1) * LANES], cos, sin, first_half)
        q_ref[:, pl.ds(j * LANES, LANES)] = (q * scale).astype(BF16)


def _ffn_body(*refs, pre, post):
    refs = list(refs)
    x = refs.pop(0)[...]
    if pre == "glu":
        ys_ref, d_ref, wg_ref, g0_ref, b0_ref = refs[:5]
        del refs[:5]
        x = _glu_ln(x, ys_ref[...], d_ref[...], wg_ref, g0_ref[...], b0_ref[...])
    elif pre == "proj":
        o_ref, wo_ref, g0_ref, b0_ref = refs[:4]
        del refs[:4]
        x = _layer_norm(ALPHA * x + _dot(o_ref[...], wo_ref[...]), g0_ref[...], b0_ref[...])
    win_ref, wout_ref, g_ref, b_ref = refs[:4]
    del refs[:4]
    y = _swiglu_ln(x, win_ref, wout_ref, g_ref[...], b_ref[...])
    if post == "q":
        wq_ref, cos_ref, sin_ref, y_ref, q_ref = refs
        _q_rope(y, wq_ref, cos_ref[...], sin_ref[...], q_ref)
    else:
        (y_ref,) = refs
    y_ref[...] = y


def _ffn_ln(x, w_in, w_out, g, b, pre=None, post=None):
    n = x.shape[0]
    tm = _row_tile(n)
    row = pl.BlockSpec((tm, D_MODEL), lambda i: (i, 0))
    vec = _const_spec((1, D_MODEL))
    args, specs = [x], [row]
    if pre is not None and pre[0] == "glu":
        args += list(pre[1:])
        specs += [row, vec, _const_spec((D_MODEL, 2 * D_MODEL)), vec, vec]
    elif pre is not None:
        args += list(pre[1:])
        specs += [row, _const_spec((D_MODEL, D_MODEL)), vec, vec]
    args += [w_in, w_out, g, b]
    specs += [_const_spec((D_MODEL, 2 * D_FF)), _const_spec((D_FF, D_MODEL)), vec, vec]
    out_specs, out_shape = [row], [jax.ShapeDtypeStruct((n, D_MODEL), F32)]
    if post is not None:
        _, w_q, pos, length = post
        table_rows = max(length, tm)
        assert table_rows % length == 0 and table_rows % tm == 0
        cos_t, sin_t = _rope_tables(pos, table_rows)
        n_table_tiles = table_rows // tm
        table = pl.BlockSpec((tm, LANES), lambda i: (i % n_table_tiles, 0))
        args += [w_q, cos_t, sin_t]
        specs += [_const_spec((D_MODEL, D_MODEL)), table, table]
        out_specs.append(row)
        out_shape.append(jax.ShapeDtypeStruct((n, D_MODEL), BF16))
    outs = pl.pallas_call(
        functools.partial(_ffn_body, pre=None if pre is None else pre[0], post=None if post is None else post[0]),
        grid=(n // tm,),
        in_specs=specs,
        out_specs=out_specs,
        out_shape=out_shape,
        compiler_params=_params(1),
        name="ffn_ln",
    )(*args)
    return outs if post is not None else outs[0]


def _s5_weights(a_re, a_im, log_dt, b_re, b_im, c_re, c_im):
    hi = lax.Precision.HIGHEST
    lam_re = jnp.minimum(a_re.astype(F32), -1e-4)
    lam_im = a_im.astype(F32)
    dt = jnp.exp(log_dt.astype(F32))[:, None]
    mag = jnp.exp(lam_re * dt)
    lb_re = mag * jnp.cos(lam_im * dt)
    lb_im = mag * jnp.sin(lam_im * dt)
    den = lam_re * lam_re + lam_im * lam_im
    n_re = lb_re - 1.0
    k_re = (n_re * lam_re + lb_im * lam_im) / den
    k_im = (lb_im * lam_re - n_re * lam_im) / den
    b_re = b_re.astype(F32)
    b_im = b_im.astype(F32)
    bb_re = k_re[..., None] * b_re - k_im[..., None] * b_im
    bb_im = k_re[..., None] * b_im + k_im[..., None] * b_re
    c_re = c_re.astype(F32)
    c_im = c_im.astype(F32)

    def power_step(carry, _):
        pr, pi = carry
        return (pr * lb_re - pi * lb_im, pr * lb_im + pi * lb_re), (pr, pi)

    _, (pw_re, pw_im) = lax.scan(power_step, (jnp.ones_like(lb_re), jnp.zeros_like(lb_re)), None,
                                 length=S5_T + 1)

    cl_re = c_re[None] * pw_re[:, :, None, :] - c_im[None] * pw_im[:, :, None, :]
    cl_im = c_re[None] * pw_im[:, :, None, :] + c_im[None] * pw_re[:, :, None, :]

    taps = (jnp.einsum('ngdp,gpc->ngcd', cl_re[:S5_T], bb_re, precision=hi)
            - jnp.einsum('ngdp,gpc->ngcd', cl_im[:S5_T], bb_im, precision=hi))
    state_w = SLAB_GROUPS * 2 * P_STATE

    lane_of = jnp.arange(LANES)[None, :] % GROUP
    repeat_lanes = (lane_of == jnp.arange(GROUP)[:, None]).astype(F32)
    w_toep = jnp.einsum('ngcd,dl->ngcl', taps, repeat_lanes, precision=hi)
    w_toep = w_toep.reshape(S5_T, N_SLABS, LANES, LANES).transpose(1, 0, 2, 3).astype(BF16)

    rev_re = pw_re[S5_T - 1::-1][:S5_T]
    rev_im = pw_im[S5_T - 1::-1][:S5_T]
    in_re = rev_re[..., None] * bb_re[None] - rev_im[..., None] * bb_im[None]
    in_im = rev_re[..., None] * bb_im[None] + rev_im[..., None] * bb_re[None]
    w_in = jnp.concatenate([in_re, in_im], axis=2)
    w_in = jnp.einsum('ck,sgpk->csgp', jnp.eye(GROUP, dtype=F32), w_in, precision=hi).transpose(1, 2, 0, 3)
    w_in = w_in.reshape(S5_T, N_SLABS, LANES, 2 * P_STATE).transpose(1, 0, 2, 3).astype(BF16)

    w_out = jnp.concatenate([cl_re[1:], -cl_im[1:]], axis=3)
    w_out = w_out.reshape(S5_T, N_SLABS, SLAB_GROUPS, GROUP, 2 * P_STATE).transpose(1, 0, 4, 2, 3)
    w_out = w_out.reshape(N_SLABS, S5_T, 2 * P_STATE, LANES).astype(BF16)

    lt_re, lt_im = pw_re[S5_T], pw_im[S5_T]
    coef = jnp.stack([jnp.concatenate([lt_re, lt_re], -1),
                      jnp.concatenate([-lt_im, lt_im], -1),
                      jnp.concatenate([lt_im, -lt_im], -1)], axis=0)
    coef = coef.reshape(3, N_SLABS, state_w).transpose(1, 0, 2)
    coef = jnp.concatenate([coef, jnp.zeros((N_SLABS, 5, state_w), F32)], axis=1)
    return w_toep, w_in, w_out, coef


def _s5_expand_weights(toep_ref, cin_ref, cout_ref, wt_ref, win_ref, wout_ref):
    group_of_row = lax.broadcasted_iota(jnp.int32, (LANES, LANES), 0) // GROUP
    group_of_col = lax.broadcasted_iota(jnp.int32, (LANES, LANES), 1) // GROUP
    zero = jnp.zeros((LANES, LANES), BF16)
    taps = [jnp.where(group_of_row == group_of_col, toep_ref[n], zero) for n in range(S5_T)]
    for s in range(S5_T):
        rows = pl.ds(s * LANES, LANES)
        for t in range(S5_T):
            wt_ref[rows, pl.ds(t * LANES, LANES)] = taps[t - s] if t >= s else zero
        tile_in = cin_ref[s]
        tile_out = cout_ref[s]
        for j in range(SLAB_GROUPS):
            win_ref[rows, pl.ds(j * LANES, LANES)] = jnp.where(group_of_row == j, tile_in, zero)
            wout_ref[pl.ds(j * LANES, LANES), rows] = jnp.where(group_of_col == j, tile_out, zero)


def _s5_body(x_ref, toep_ref, cin_ref, cout_ref, coef_ref, h0_ref, y_ref, hfin_ref,
             wt_ref, win_ref, wout_ref, s_ref, hst_ref, *, rows, n_chunks, bt):
    @pl.when(pl.program_id(1) == 0)
    def _():
        _s5_expand_weights(toep_ref, cin_ref, cout_ref, wt_ref, win_ref, wout_ref)

    xcat = jnp.concatenate([x_ref[pl.ds(s, rows, stride=S5_T), :].astype(BF16) for s in range(S5_T)], axis=1)
    s_all = _dot(xcat, win_ref[...])
    for j in range(SLAB_GROUPS):
        sj = s_all[:, j * LANES:(j + 1) * LANES]
        sj_t = pltpu.roll(sj, P_STATE, 1)
        for b in range(bt):
            chunk_major = pl.ds(b, n_chunks, stride=bt)
            s_ref[j, chunk_major, :] = sj[b * n_chunks:(b + 1) * n_chunks]
            s_ref[SLAB_GROUPS + j, chunk_major, :] = sj_t[b * n_chunks:(b + 1) * n_chunks]
    coef = coef_ref[...]
    h0 = h0_ref[...]

    def lanes(v, j):
        return v[:, j * LANES:(j + 1) * LANES]

    def step(k, carry):
        hs, hts = carry
        idx = pl.ds(k * bt, bt)
        new_h, new_ht = [], []
        for j in range(SLAB_GROUPS):
            hst_ref[j, idx, :] = hs[j]
            c_self, c_swap, c_swap_t = lanes(coef[0:1], j), lanes(coef[1:2], j), lanes(coef[2:3], j)
            new_h.append(c_self * hs[j] + c_swap * hts[j] + s_ref[j, idx, :])
            new_ht.append(c_self * hts[j] + c_swap_t * hs[j] + s_ref[SLAB_GROUPS + j, idx, :])
        return tuple(new_h), tuple(new_ht)

    init = (tuple(lanes(h0[0], j) for j in range(SLAB_GROUPS)), tuple(lanes(h0[1], j) for j in range(SLAB_GROUPS)))
    carry = init
    for k in range(n_chunks):
        carry = step(k, carry)
    hs, _ = carry
    hfin_ref[...] = jnp.concatenate(hs, axis=1)
    hst = jnp.concatenate(
        [jnp.concatenate([hst_ref[j, pl.ds(b, n_chunks, stride=bt), :] for b in range(bt)], axis=0)
         for j in range(SLAB_GROUPS)], axis=1).astype(BF16)
    for t in range(0, S5_T, 2):
        cols = pl.ds(t * LANES, 2 * LANES)
        k_used = (t + 2) * LANES
        y = _dot(xcat[:, :k_used], wt_ref[pl.ds(0, k_used), cols]) + _dot(hst, wout_ref[:, cols])
        y_ref[pl.ds(t, rows, stride=S5_T), :] = y[:, :LANES]
        y_ref[pl.ds(t + 1, rows, stride=S5_T), :] = y[:, LANES:]


def _s5_batch_tile(nb, n_chunks):
    bt = nb
    while bt * n_chunks > S5_ROWS and bt % 2 == 0:
        bt //= 2
    return bt


def _s5_core(u, nb, h0_re, h0_im, weights):
    w_toep, w_in, w_out, coef = weights
    n = u.shape[0]
    length = n // nb
    n_chunks = length // S5_T
    bt = _s5_batch_tile(nb, n_chunks)
    n_bt = nb // bt
    rows = bt * n_chunks
    wide = S5_T * LANES
    sw = SLAB_GROUPS * 2 * P_STATE

    def slab_state(re, im):
        v = jnp.concatenate([re, im], axis=-1).astype(F32)
        return v.reshape(n_bt, bt, N_SLABS, sw).transpose(2, 0, 1, 3)

    h0 = jnp.stack([slab_state(h0_re, h0_im), slab_state(h0_im, h0_re)], axis=2)

    def per_slab(shape):
        return pl.BlockSpec((None,) + shape, lambda s, i: (s,) + (0,) * len(shape), pipeline_mode=pl.Buffered(1))

    tokens = pl.BlockSpec((bt * length, LANES), lambda s, i: (i, s))
    y, hfin = pl.pallas_call(
        functools.partial(_s5_body, rows=rows, n_chunks=n_chunks, bt=bt),
        grid=(N_SLABS, n_bt),
        in_specs=[tokens, per_slab((S5_T, LANES, LANES)), per_slab((S5_T, LANES, 2 * P_STATE)),
                  per_slab((S5_T, 2 * P_STATE, LANES)), per_slab((8, sw)),
                  pl.BlockSpec((None, None, 2, bt, sw), lambda s, i: (s, i, 0, 0, 0))],
        out_specs=[tokens, pl.BlockSpec((None, None, bt, sw), lambda s, i: (s, i, 0, 0))],
        out_shape=[jax.ShapeDtypeStruct((n, D_MODEL), F32), jax.ShapeDtypeStruct((N_SLABS, n_bt, bt, sw), F32)],
        scratch_shapes=[pltpu.VMEM((wide, wide), BF16), pltpu.VMEM((wide, sw), BF16), pltpu.VMEM((sw, wide), BF16),
                        pltpu.VMEM((2 * SLAB_GROUPS, rows, LANES), F32), pltpu.VMEM((SLAB_GROUPS, rows, LANES), F32)],
        compiler_params=_params(2),
        name="s5_core",
    )(u, w_toep, w_in, w_out, coef, h0)
    hfin = hfin.transpose(1, 2, 0, 3).reshape(nb, N_GROUPS, 2 * P_STATE)
    return y, hfin[..., :P_STATE], hfin[..., P_STATE:]


def _rope_tables(pos, rows):
    half = HEAD_DIM // 2
    inv = ROPE_THETA ** (-jnp.arange(half, dtype=F32) / half)
    ang = pos.astype(F32)[:, None] * inv[None, :]
    cos = jnp.cos(ang)
    sin = jnp.sin(ang)
    cos_t = jnp.concatenate([cos, cos, cos, cos], axis=-1)
    sin_t = jnp.concatenate([-sin, sin, -sin, sin], axis=-1)
    reps = rows // pos.shape[0]
    return jnp.tile(cos_t, (reps, 1)), jnp.tile(sin_t, (reps, 1))


def _rope_slab(v, cos, sin, first_half):
    partner = jnp.where(first_half, pltpu.roll(v, LANES - HEAD_DIM // 2, 1), pltpu.roll(v, HEAD_DIM // 2, 1))
    return v * cos + partner * sin


def _first_half_mask(rows):
    lane = lax.broadcasted_iota(jnp.int32, (rows, LANES), 1)
    return (lane % HEAD_DIM) < (HEAD_DIM // 2)


def _kv_rope_body(x_ref, wk_ref, wv_ref, cos_ref, sin_ref, kt_ref, ktb_ref, v_ref, vb_ref):
    xb = x_ref[...].astype(BF16)
    tm = xb.shape[0]
    half = HEAD_DIM // 2
    cos = cos_ref[...]
    sin = sin_ref[...]
    kt = _dot_nt(wk_ref[...], xb)
    for h in range(N_HEADS):
        for mp in range(2):
            lo = (2 * h + mp) * HEAD_DIM
            x1 = kt[lo:lo + half]
            x2 = kt[lo + half:lo + HEAD_DIM]
            r1 = x1 * cos - x2 * sin
            r2 = x1 * sin + x2 * cos
            kt_ref[h, mp, pl.ds(0, half), :] = r1
            kt_ref[h, mp, pl.ds(half, half), :] = r2
            ktb_ref[h, pl.ds(mp * HEAD_DIM, half), :] = r1.astype(BF16)
            ktb_ref[h, pl.ds(mp * HEAD_DIM + half, half), :] = r2.astype(BF16)
    v_all = _dot(xb, wv_ref[...])
    vb_ref[...] = v_all.astype(BF16)
    for h in range(N_HEADS):
        v_ref[pl.ds(h, tm, stride=N_HEADS), :] = v_all[:, h * V_DIM:(h + 1) * V_DIM]


def _kv_rope(x, nb, w_kv, pos):
    n = x.shape[0]
    length = n // nb
    tm = min(ROW_TILE, length)
    tiles = length // tm
    half = HEAD_DIM // 2
    inv = ROPE_THETA ** (-jnp.arange(half, dtype=F32) / half)
    ang = inv[:, None] * pos.astype(F32)[None, :]
    wk_t = w_kv[:, :D_MODEL].T
    w_v = w_kv[:, D_MODEL:]
    table = pl.BlockSpec((half, tm), lambda i: (0, i % tiles))
    return pl.pallas_call(
        _kv_rope_body,
        grid=(n // tm,),
        in_specs=[pl.BlockSpec((tm, D_MODEL), lambda i: (i, 0)), _const_spec((D_MODEL, D_MODEL)),
                  _const_spec((D_MODEL, D_MODEL)), table, table],
        out_specs=[pl.BlockSpec((None, N_HEADS, 2, HEAD_DIM, tm), lambda i: (i // tiles, 0, 0, 0, i % tiles)),
                   pl.BlockSpec((None, N_HEADS, V_DIM, tm), lambda i: (i // tiles, 0, 0, i % tiles)),
                   pl.BlockSpec((tm * N_HEADS, V_DIM), lambda i: (i, 0)),
                   pl.BlockSpec((tm, D_MODEL), lambda i: (i, 0))],
        out_shape=[jax.ShapeDtypeStruct((nb, N_HEADS, 2, HEAD_DIM, length), F32),
                   jax.ShapeDtypeStruct((nb, N_HEADS, V_DIM, length), BF16),
                   jax.ShapeDtypeStruct((n * N_HEADS, V_DIM), F32),
                   jax.ShapeDtypeStruct((n, D_MODEL), BF16)],
        compiler_params=_params(1),
        name="kv_rope",
    )(x, wk_t, w_v, jnp.cos(ang), jnp.sin(ang))


def _lam_value(lam_ref, layer_idx):
    lv = lam_ref[...]
    s01 = jnp.sum(lv[0:1, :] * lv[1:2, :], axis=-1, keepdims=True)
    s23 = jnp.sum(lv[2:3, :] * lv[3:4, :], axis=-1, keepdims=True)
    lam_init = 0.8 - 0.6 * math.exp(-0.3 * layer_idx)
    return jnp.exp(s01) - jnp.exp(s23) + lam_init, lam_init


def _diff_attend(q, segments, lam, lam_init, subln_g):
    lane = lax.broadcasted_iota(jnp.int32, q.shape, 1)
    zero = jnp.zeros_like(q)
    q1 = jnp.where(lane < HEAD_DIM, q, zero)
    q2 = jnp.where(lane >= HEAD_DIM, q, zero)
    maps = []
    for qm in (q1, q2):
        scores = []
        for kt, _, mask in segments:
            s = _dot(qm, kt)
            scores.append(s if mask is None else jnp.where(mask, s, NEG_INF))
        m = functools.reduce(jnp.maximum, [jnp.max(s, axis=-1, keepdims=True) for s in scores])
        es = [jnp.exp2(s - m) for s in scores]
        denom = functools.reduce(jnp.add, [jnp.sum(e, axis=-1, keepdims=True) for e in es])
        acc = functools.reduce(jnp.add, [_dot(e.astype(BF16), v) for e, (_, v, _) in zip(es, segments)])
        maps.append(acc / denom)
    out = maps[0] - lam * maps[1]
    ms = jnp.mean(out * out, axis=-1, keepdims=True)
    return out * lax.rsqrt(ms + RMS_EPS) * subln_g * (1.0 - lam_init)


def _attn_prompt_body(q_ref, kt_ref, v_ref, lam_ref, g_ref, o_ref, *, length, tq, layer_idx):
    lam, lam_init = _lam_value(lam_ref, layer_idx)
    g = g_ref[...]
    r = lax.broadcasted_iota(jnp.int32, (tq, tq), 0) // CHUNK
    c = lax.broadcasted_iota(jnp.int32, (tq, tq), 1) // CHUNK
    diag_mask = c <= r
    for i in range(length // tq):
        rows = pl.ds(i * tq, tq)
        for hh in range(ATTN_HEADS):
            cols = pl.ds(hh * V_DIM, V_DIM)
            segments = []
            if i > 0:
                segments.append((kt_ref[hh, :, pl.ds(0, i * tq)], v_ref[pl.ds(0, i * tq), cols], None))
            segments.append((kt_ref[hh, :, rows], v_ref[rows, cols], diag_mask))
            o_ref[rows, cols] = _diff_attend(q_ref[rows, cols], segments, lam, lam_init, g).astype(BF16)


def _attn_prompt(q, ktb, vb, attn_lam, subln_g, layer_idx):
    nb, length, _ = q.shape
    tq = min(Q_TILE, length)
    head = pl.BlockSpec((None, length, ATTN_HEADS * V_DIM), lambda b, h: (b, 0, h))
    head_t = pl.BlockSpec((None, ATTN_HEADS, V_DIM, length), lambda b, h: (b, h, 0, 0))
    return pl.pallas_call(
        functools.partial(_attn_prompt_body, length=length, tq=tq, layer_idx=layer_idx),
        grid=(nb, N_HEADS // ATTN_HEADS),
        in_specs=[head, head_t, head, _const_spec((4, HEAD_DIM)), _const_spec((1, V_DIM))],
        out_specs=head,
        out_shape=jax.ShapeDtypeStruct((nb, length, D_MODEL), BF16),
        compiler_params=_params(2),
        name="attn_prompt",
    )(q, ktb, vb, attn_lam, subln_g)


def _attn_sample_body(q_ref, kp_ref, vp_ref, kt_ref, v_ref, lam_ref, g_ref, o_ref, m_ref, l_ref, acc_ref, *, layer_idx, tp):
    pt = pl.program_id(1)
    n_q = q_ref.shape[0]
    zero = jnp.zeros((n_q, HEAD_DIM), BF16)

    def head_scores(h, k1, k2):
        q1 = q_ref[:, h * V_DIM:h * V_DIM + HEAD_DIM]
        q2 = q_ref[:, h * V_DIM + HEAD_DIM:(h + 1) * V_DIM]
        return _dot(jnp.concatenate([q1, zero], axis=0), k1) + _dot(jnp.concatenate([zero, q2], axis=0), k2)

    def accumulate(keys_of, values_of, first):
        s = jnp.concatenate([head_scores(h, *keys_of(h)) for h in range(N_HEADS)], axis=0)
        m_tile = jnp.max(s, axis=-1, keepdims=True)
        if first:
            m_new = m_tile
        else:
            m_old = m_ref[...]
            m_new = jnp.maximum(m_old, m_tile)
            alpha = jnp.exp2(m_old - m_new)
        e = jnp.exp2(s - m_new)
        l_tile = jnp.sum(e, axis=-1, keepdims=True)
        eb = e.astype(BF16)
        pv = jnp.concatenate([_dot(eb[2 * n_q * h:2 * n_q * (h + 1)], values_of(h)) for h in range(N_HEADS)], axis=0)
        m_ref[...] = m_new
        l_ref[...] = l_tile if first else alpha * l_ref[...] + l_tile
        acc_ref[...] = pv if first else alpha * acc_ref[...] + pv

    @pl.when(pt == 0)
    def _():
        accumulate(lambda h: (kt_ref[h, pl.ds(0, HEAD_DIM), :], kt_ref[h, pl.ds(HEAD_DIM, HEAD_DIM), :]),
                   lambda h: v_ref[:, h * V_DIM:(h + 1) * V_DIM], True)

    accumulate(lambda h: (kp_ref[h, 0].astype(BF16), kp_ref[h, 1].astype(BF16)),
               lambda h: vp_ref[pl.ds(h, tp, stride=N_HEADS), :].astype(BF16), False)

    @pl.when(pt == pl.num_programs(1) - 1)
    def _():
        lam, lam_init = _lam_value(lam_ref, layer_idx)
        g = g_ref[...]
        for h in range(N_HEADS):
            r1 = pl.ds(2 * n_q * h, n_q)
            r2 = pl.ds(2 * n_q * h + n_q, n_q)
            out = acc_ref[r1, :] / l_ref[r1, :] - acc_ref[r2, :] * (lam / l_ref[r2, :])
            ms = jnp.mean(out * out, axis=-1, keepdims=True)
            o_ref[:, h * V_DIM:(h + 1) * V_DIM] = (out * lax.rsqrt(ms + RMS_EPS) * g * (1.0 - lam_init)).astype(BF16)


def _attn_sample(q, cache_k, cache_v, ktb, vb, attn_lam, subln_g, layer_idx):
    nb, length, _ = q.shape
    past = cache_k.shape[1]
    assert past % CHUNK == 0 and length <= CHUNK
    tp = PAST_TILE if past % PAST_TILE == 0 else past
    new = pl.BlockSpec((None, length, D_MODEL), lambda b, p: (b, 0, 0))
    hm = 2 * N_HEADS
    return pl.pallas_call(
        functools.partial(_attn_sample_body, layer_idx=layer_idx, tp=tp),
        grid=(nb, past // tp),
        in_specs=[new,
                  pl.BlockSpec((None, N_HEADS, 2, HEAD_DIM, tp), lambda b, p: (b, 0, 0, 0, p)),
                  pl.BlockSpec((None, tp * N_HEADS, V_DIM), lambda b, p: (b, p, 0)),
                  pl.BlockSpec((None, N_HEADS, V_DIM, length), lambda b, p: (b, 0, 0, 0)),
                  new, _const_spec((4, HEAD_DIM)), _const_spec((1, V_DIM))],
        out_specs=new,
        out_shape=jax.ShapeDtypeStruct((nb, length, D_MODEL), BF16),
        scratch_shapes=[pltpu.VMEM((hm * length, 1), F32), pltpu.VMEM((hm * length, 1), F32),
                        pltpu.VMEM((hm * length, V_DIM), F32)],
        compiler_params=_params(2),
        name="attn_sample",
    )(q, cache_k.transpose(0, 2, 3, 4, 1), cache_v.reshape(nb, past * N_HEADS, V_DIM), ktb, vb, attn_lam, subln_g)


def _run_trunk(x, pos, h0_re, h0_im, cache_k, cache_v, w):
    nb, length, _ = x.shape
    n = nb * length
    assert length % S5_T == 0
    ln_g, ln_b = w['ln_g'], w['ln_b']

    def vec(v):
        return v.reshape(1, D_MODEL).astype(F32)

    def ffn(t, l, j, ln_idx, **fused):
        return _ffn_ln(t, w['ffn_w_in'][l, j], w['ffn_w_out'][l, j], vec(ln_g[l, ln_idx]), vec(ln_b[l, ln_idx]), **fused)

    t = x.reshape(n, D_MODEL)
    new_re, new_im = [], []
    kt = v = None
    for l in range(DEPTH):
        if l == N_A:
            kt, ktb, v, vb = _kv_rope(t, nb, w['attn_w_kv'], pos)
        if l < N_A:
            t = ffn(t, l, 0, 0)
            ys, hr, hi = _s5_core(t, nb, h0_re[l], h0_im[l], w['ssm'][l])
            new_re.append(hr)
            new_im.append(hi)
            mixer = ("glu", ys, vec(w['ssm_d'][l]), w['ssm_w_glu'][l], vec(ln_g[l, 1]), vec(ln_b[l, 1]))
        else:
            j = l - N_A
            t, q = ffn(t, l, 0, 0, post=("q", w['attn_w_q'][j], pos, length))
            q = q.reshape(nb, length, D_MODEL)
            vb3 = vb.reshape(nb, length, D_MODEL)
            lam_p = w['attn_lam'][j].astype(F32)
            sub_g = w['attn_subln_g'][j].reshape(1, V_DIM).astype(F32)
            if cache_k is None:
                o = _attn_prompt(q, ktb, vb3, lam_p, sub_g, l)
            else:
                o = _attn_sample(q, cache_k, cache_v, ktb, vb3, lam_p, sub_g, l)
            mixer = ("proj", o.reshape(n, D_MODEL), w['attn_w_o'][j], vec(ln_g[l, 1]), vec(ln_b[l, 1]))
        t = ffn(t, l, 1, 2, pre=mixer)
    y = t.reshape(nb, length, D_MODEL)
    p_k = kt.transpose(0, 4, 1, 2, 3)
    p_v = v.reshape(nb, length, N_HEADS, V_DIM)
    return y, jnp.stack(new_re), jnp.stack(new_im), p_k, p_v


def kernel(x_prompt, x_sample, cache_k, cache_v, state_ssm_re, state_ssm_im,
           ln_g, ln_b, ffn_w_in, ffn_w_out,
           ssm_a_re, ssm_a_im, ssm_log_dt, ssm_b_re, ssm_b_im, ssm_c_re, ssm_c_im, ssm_d, ssm_w_glu,
           attn_w_q, attn_w_kv, attn_lam, attn_subln_g, attn_w_o):
    w = {'ln_g': ln_g, 'ln_b': ln_b,
         'ffn_w_in': ffn_w_in.astype(BF16), 'ffn_w_out': ffn_w_out.astype(BF16),
         'ssm': [_s5_weights(ssm_a_re[l], ssm_a_im[l], ssm_log_dt[l], ssm_b_re[l], ssm_b_im[l],
                             ssm_c_re[l], ssm_c_im[l]) for l in range(N_A)],
         'ssm_d': ssm_d, 'ssm_w_glu': ssm_w_glu.astype(BF16),
         'attn_w_q': attn_w_q.astype(BF16), 'attn_w_kv': attn_w_kv.astype(BF16), 'attn_lam': attn_lam,
         'attn_subln_g': attn_subln_g, 'attn_w_o': attn_w_o.astype(BF16)}
    b_p, l_p, _ = x_prompt.shape
    pos_p = jnp.arange(l_p, dtype=jnp.int32)
    h0 = jnp.zeros((N_A, b_p, N_GROUPS, P_STATE), F32)
    y_p, p_re, p_im, p_k, p_v = _run_trunk(x_prompt, pos_p, h0, h0, None, None, w)
    past = cache_k.shape[1]
    pos_s = past + jnp.arange(x_sample.shape[1], dtype=jnp.int32)
    y_s, s_re, s_im, s_k, s_v = _run_trunk(x_sample, pos_s, state_ssm_re, state_ssm_im, cache_k, cache_v, w)
    return (y_p, y_s, p_re, p_im, p_k, p_v, s_re, s_im, s_k, s_v)
```

```python
import functools
import math

import jax
import jax.numpy as jnp
from jax import lax
from jax.experimental import pallas as pl
from jax.experimental.pallas import tpu as pltpu

F32 = jnp.float32
BF16 = jnp.bfloat16

D_MODEL = 1024
DEPTH = 2
CHUNK = 64
N_A = DEPTH // 2
GROUP = 16
N_GROUPS = D_MODEL // GROUP
P_STATE = 64
HEAD_DIM = 64
V_DIM = 2 * HEAD_DIM
N_HEADS = D_MODEL // V_DIM
D_FF = ((8 * D_MODEL // 3 + 127) // 128) * 128
ROPE_THETA = 10000.0
ALPHA = (2 * DEPTH) ** 0.25
LN_EPS = 1e-5
RMS_EPS = 1e-5
NEG_INF = -1e30
LOG2_E = math.log2(math.e)

LANES = 128
ROW_TILE = 512
FF_TILE = 256
S5_T = 16
SLAB_GROUPS = LANES // GROUP
N_SLABS = N_GROUPS // SLAB_GROUPS
S5_ROWS = 512
Q_TILE = 512
ATTN_HEADS = 2
PAST_TILE = 1024
VMEM_LIMIT = 56 * 1024 * 1024


def _params(n_axes):
    return pltpu.CompilerParams(dimension_semantics=("arbitrary",) * n_axes, vmem_limit_bytes=VMEM_LIMIT)


def _row_tile(n):
    return ROW_TILE if n % ROW_TILE == 0 else n


def _const_spec(shape):
    return pl.BlockSpec(shape, lambda *_: (0,) * len(shape), pipeline_mode=pl.Buffered(1))


def _layer_norm(v, g, b):
    mu = jnp.mean(v, axis=-1, keepdims=True)
    c = v - mu
    var = jnp.mean(c * c, axis=-1, keepdims=True)
    return c * lax.rsqrt(var + LN_EPS) * g + b


def _sigmoid(v):
    return 0.5 * jnp.tanh(0.5 * v) + 0.5


def _gelu_tanh(v):
    return 0.5 * v * (1.0 + jnp.tanh(math.sqrt(2.0 / math.pi) * (v + 0.044715 * (v * v * v))))


def _dot(a, b):
    return jnp.dot(a, b, preferred_element_type=F32)


def _dot_nt(a, b):
    return lax.dot_general(a, b, (((1,), (1,)), ((), ())), preferred_element_type=F32)


def _swiglu_ln(x, win_ref, wout_ref, g, b):
    xb = x.astype(BF16)
    acc = None
    for c in range(D_FF // FF_TILE):
        a = _dot(xb, win_ref[:, pl.ds(c * FF_TILE, FF_TILE)])
        gate = _dot(xb, win_ref[:, pl.ds(D_FF + c * FF_TILE, FF_TILE)])
        h = (a * _sigmoid(a) * gate).astype(BF16)
        part = _dot(h, wout_ref[pl.ds(c * FF_TILE, FF_TILE), :])
        acc = part if acc is None else acc + part
    return _layer_norm(ALPHA * x + 0.5 * acc, g, b)


def _glu_ln(x, ys, d_skip, wg_ref, g, b):
    z = _gelu_tanh(ys + d_skip * x).astype(BF16)
    out = _dot(z, wg_ref[:, pl.ds(0, D_MODEL)])
    gate = _dot(z, wg_ref[:, pl.ds(D_MODEL, D_MODEL)])
    return _layer_norm(ALPHA * x + out * _sigmoid(gate), g, b)


def _q_rope(x, wq_ref, cos, sin, q_ref):
    xb = x.astype(BF16)
    first_half = _first_half_mask(xb.shape[0])
    scale = HEAD_DIM ** -0.5 * LOG2_E
    q_all = _dot(xb, wq_ref[...])
    for j in range(D_MODEL // LANES):
        q = _rope_slab(q_all[:, j * LANES:(j + 1) * LANES], cos, sin, first_half)
        q_ref[:, pl.ds(j * LANES, LANES)] = (q * scale).astype(BF16)


def _ffn_body(*refs, pre, post):
    refs = list(refs)
    x = refs.pop(0)[...]
    if pre == "glu":
        ys_ref, d_ref, wg_ref, g0_ref, b0_ref = refs[:5]
        del refs[:5]
        x = _glu_ln(x, ys_ref[...], d_ref[...], wg_ref, g0_ref[...], b0_ref[...])
    elif pre == "proj":
        o_ref, wo_ref, g0_ref, b0_ref = refs[:4]
        del refs[:4]
        x = _layer_norm(ALPHA * x + _dot(o_ref[...], wo_ref[...]), g0_ref[...], b0_ref[...])
    win_ref, wout_ref, g_ref, b_ref = refs[:4]
    del refs[:4]
    y = _swiglu_ln(x, win_ref, wout_ref, g_ref[...], b_ref[...])
    if post == "q":
        wq_ref, cos_ref, sin_ref, y_ref, q_ref = refs
        _q_rope(y, wq_ref, cos_ref[...], sin_ref[...], q_ref)
    else:
        (y_ref,) = refs
    y_ref[...] = y


def _ffn_ln(x, w_in, w_out, g, b, pre=None, post=None):
    n = x.shape[0]
    tm = _row_tile(n)
    row = pl.BlockSpec((tm, D_MODEL), lambda i: (i, 0))
    vec = _const_spec((1, D_MODEL))
    args, specs = [x], [row]
    if pre is not None and pre[0] == "glu":
        args += list(pre[1:])
        specs += [row, vec, _const_spec((D_MODEL, 2 * D_MODEL)), vec, vec]
    elif pre is not None:
        args += list(pre[1:])
        specs += [row, _const_spec((D_MODEL, D_MODEL)), vec, vec]
    args += [w_in, w_out, g, b]
    specs += [_const_spec((D_MODEL, 2 * D_FF)), _const_spec((D_FF, D_MODEL)), vec, vec]
    out_specs, out_shape = [row], [jax.ShapeDtypeStruct((n, D_MODEL), F32)]
    if post is not None:
        _, w_q, pos, length = post
        table_rows = max(length, tm)
        assert table_rows % length == 0 and table_rows % tm == 0
        cos_t, sin_t = _rope_tables(pos, table_rows)
        n_table_tiles = table_rows // tm
        table = pl.BlockSpec((tm, LANES), lambda i: (i % n_table_tiles, 0))
        args += [w_q, cos_t, sin_t]
        specs += [_const_spec((D_MODEL, D_MODEL)), table, table]
        out_specs.append(row)
        out_shape.append(jax.ShapeDtypeStruct((n, D_MODEL), BF16))
    outs = pl.pallas_call(
        functools.partial(_ffn_body, pre=None if pre is None else pre[0], post=None if post is None else post[0]),
        grid=(n // tm,),
        in_specs=specs,
        out_specs=out_specs,
        out_shape=out_shape,
        compiler_params=_params(1),
        name="ffn_ln",
    )(*args)
    return outs if post is not None else outs[0]


def _s5_weights(a_re, a_im, log_dt, b_re, b_im, c_re, c_im):
    hi = lax.Precision.HIGHEST
    lam_re = jnp.minimum(a_re.astype(F32), -1e-4)
    lam_im = a_im.astype(F32)
    dt = jnp.exp(log_dt.astype(F32))[:, None]
    mag = jnp.exp(lam_re * dt)
    lb_re = mag * jnp.cos(lam_im * dt)
    lb_im = mag * jnp.sin(lam_im * dt)
    den = lam_re * lam_re + lam_im * lam_im
    n_re = lb_re - 1.0
    k_re = (n_re * lam_re + lb_im * lam_im) / den
    k_im = (lb_im * lam_re - n_re * lam_im) / den
    b_re = b_re.astype(F32)
    b_im = b_im.astype(F32)
    bb_re = k_re[..., None] * b_re - k_im[..., None] * b_im
    bb_im = k_re[..., None] * b_im + k_im[..., None] * b_re
    c_re = c_re.astype(F32)
    c_im = c_im.astype(F32)

    def power_step(carry, _):
        pr, pi = carry
        return (pr * lb_re - pi * lb_im, pr * lb_im + pi * lb_re), (pr, pi)

    _, (pw_re, pw_im) = lax.scan(power_step, (jnp.ones_like(lb_re), jnp.zeros_like(lb_re)), None,
                                 length=S5_T + 1)

    cl_re = c_re[None] * pw_re[:, :, None, :] - c_im[None] * pw_im[:, :, None, :]
    cl_im = c_re[None] * pw_im[:, :, None, :] + c_im[None] * pw_re[:, :, None, :]

    taps = (jnp.einsum('ngdp,gpc->ngcd', cl_re[:S5_T], bb_re, precision=hi)
            - jnp.einsum('ngdp,gpc->ngcd', cl_im[:S5_T], bb_im, precision=hi))
    state_w = SLAB_GROUPS * 2 * P_STATE

    lane_of = jnp.arange(LANES)[None, :] % GROUP
    repeat_lanes = (lane_of == jnp.arange(GROUP)[:, None]).astype(F32)
    w_toep = jnp.einsum('ngcd,dl->ngcl', taps, repeat_lanes, precision=hi)
    w_toep = w_toep.reshape(S5_T, N_SLABS, LANES, LANES).transpose(1, 0, 2, 3).astype(BF16)

    rev_re = pw_re[S5_T - 1::-1][:S5_T]
    rev_im = pw_im[S5_T - 1::-1][:S5_T]
    in_re = rev_re[..., None] * bb_re[None] - rev_im[..., None] * bb_im[None]
    in_im = rev_re[..., None] * bb_im[None] + rev_im[..., None] * bb_re[None]
    w_in = jnp.concatenate([in_re, in_im], axis=2)
    w_in = jnp.einsum('ck,sgpk->csgp', jnp.eye(GROUP, dtype=F32), w_in, precision=hi).transpose(1, 2, 0, 3)
    w_in = w_in.reshape(S5_T, N_SLABS, LANES, 2 * P_STATE).transpose(1, 0, 2, 3).astype(BF16)

    w_out = jnp.concatenate([cl_re[1:], -cl_im[1:]], axis=3)
    w_out = w_out.reshape(S5_T, N_SLABS, SLAB_GROUPS, GROUP, 2 * P_STATE).transpose(1, 0, 4, 2, 3)
    w_out = w_out.reshape(N_SLABS, S5_T, 2 * P_STATE, LANES).astype(BF16)

    lt_re, lt_im = pw_re[S5_T], pw_im[S5_T]
    coef = jnp.stack([jnp.concatenate([lt_re, lt_re], -1),
                      jnp.concatenate([-lt_im, lt_im], -1),
                      jnp.concatenate([lt_im, -lt_im], -1)], axis=0)
    coef = coef.reshape(3, N_SLABS, state_w).transpose(1, 0, 2)
    coef = jnp.concatenate([coef, jnp.zeros((N_SLABS, 5, state_w), F32)], axis=1)
    return w_toep, w_in, w_out, coef


def _s5_expand_weights(toep_ref, cin_ref, cout_ref, wt_ref, win_ref, wout_ref):
    group_of_row = lax.broadcasted_iota(jnp.int32, (LANES, LANES), 0) // GROUP
    group_of_col = lax.broadcasted_iota(jnp.int32, (LANES, LANES), 1) // GROUP
    zero = jnp.zeros((LANES, LANES), BF16)
    taps = [jnp.where(group_of_row == group_of_col, toep_ref[n], zero) for n in range(S5_T)]
    for s in range(S5_T):
        rows = pl.ds(s * LANES, LANES)
        for t in range(S5_T):
            wt_ref[rows, pl.ds(t * LANES, LANES)] = taps[t - s] if t >= s else zero
        tile_in = cin_ref[s]
        tile_out = cout_ref[s]
        for j in range(SLAB_GROUPS):
            win_ref[rows, pl.ds(j * LANES, LANES)] = jnp.where(group_of_row == j, tile_in, zero)
            wout_ref[pl.ds(j * LANES, LANES), rows] = jnp.where(group_of_col == j, tile_out, zero)


def _s5_body(x_ref, toep_ref, cin_ref, cout_ref, coef_ref, h0_ref, y_ref, hfin_ref,
             wt_ref, win_ref, wout_ref, s_ref, hst_ref, *, rows, n_chunks, bt):
    @pl.when(pl.program_id(1) == 0)
    def _():
        _s5_expand_weights(toep_ref, cin_ref, cout_ref, wt_ref, win_ref, wout_ref)

    xcat = jnp.concatenate([x_ref[pl.ds(s, rows, stride=S5_T), :].astype(BF16) for s in range(S5_T)], axis=1)
    s_all = _dot(xcat, win_ref[...])
    for j in range(SLAB_GROUPS):
        sj = s_all[:, j * LANES:(j + 1) * LANES]
        sj_t = pltpu.roll(sj, P_STATE, 1)
        for b in range(bt):
            chunk_major = pl.ds(b, n_chunks, stride=bt)
            s_ref[j, chunk_major, :] = sj[b * n_chunks:(b + 1) * n_chunks]
            s_ref[SLAB_GROUPS + j, chunk_major, :] = sj_t[b * n_chunks:(b + 1) * n_chunks]
    coef = coef_ref[...]
    h0 = h0_ref[...]

    def lanes(v, j):
        return v[:, j * LANES:(j + 1) * LANES]

    def step(k, carry):
        hs, hts = carry
        idx = pl.ds(k * bt, bt)
        new_h, new_ht = [], []
        for j in range(SLAB_GROUPS):
            hst_ref[j, idx, :] = hs[j]
            c_self, c_swap, c_swap_t = lanes(coef[0:1], j), lanes(coef[1:2], j), lanes(coef[2:3], j)
            new_h.append(c_self * hs[j] + c_swap * hts[j] + s_ref[j, idx, :])
            new_ht.append(c_self * hts[j] + c_swap_t * hs[j] + s_ref[SLAB_GROUPS + j, idx, :])
        return tuple(new_h), tuple(new_ht)

    init = (tuple(lanes(h0[0], j) for j in range(SLAB_GROUPS)), tuple(lanes(h0[1], j) for j in range(SLAB_GROUPS)))
    carry = init
    for k in range(n_chunks):
        carry = step(k, carry)
    hs, _ = carry
    hfin_ref[...] = jnp.concatenate(hs, axis=1)
    hst = jnp.concatenate(
        [jnp.concatenate([hst_ref[j, pl.ds(b, n_chunks, stride=bt), :] for b in range(bt)], axis=0)
         for j in range(SLAB_GROUPS)], axis=1).astype(BF16)
    for t in range(0, S5_T, 2):
        cols = pl.ds(t * LANES, 2 * LANES)
        k_used = (t + 2) * LANES
        y = _dot(xcat[:, :k_used], wt_ref[pl.ds(0, k_used), cols]) + _dot(hst, wout_ref[:, cols])
        y_ref[pl.ds(t, rows, stride=S5_T), :] = y[:, :LANES]
        y_ref[pl.ds(t + 1, rows, stride=S5_T), :] = y[:, LANES:]


def _s5_batch_tile(nb, n_chunks):
    bt = nb
    while bt * n_chunks > S5_ROWS and bt % 2 == 0:
        bt //= 2
    return bt


def _s5_core(u, nb, h0_re, h0_im, weights):
    w_toep, w_in, w_out, coef = weights
    n = u.shape[0]
    length = n // nb
    n_chunks = length // S5_T
    bt = _s5_batch_tile(nb, n_chunks)
    n_bt = nb // bt
    rows = bt * n_chunks
    wide = S5_T * LANES
    sw = SLAB_GROUPS * 2 * P_STATE

    def slab_state(re, im):
        v = jnp.concatenate([re, im], axis=-1).astype(F32)
        return v.reshape(n_bt, bt, N_SLABS, sw).transpose(2, 0, 1, 3)

    h0 = jnp.stack([slab_state(h0_re, h0_im), slab_state(h0_im, h0_re)], axis=2)

    def per_slab(shape):
        return pl.BlockSpec((None,) + shape, lambda s, i: (s,) + (0,) * len(shape), pipeline_mode=pl.Buffered(1))

    tokens = pl.BlockSpec((bt * length, LANES), lambda s, i: (i, s))
    y, hfin = pl.pallas_call(
        functools.partial(_s5_body, rows=rows, n_chunks=n_chunks, bt=bt),
        grid=(N_SLABS, n_bt),
        in_specs=[tokens, per_slab((S5_T, LANES, LANES)), per_slab((S5_T, LANES, 2 * P_STATE)),
                  per_slab((S5_T, 2 * P_STATE, LANES)), per_slab((8, sw)),
                  pl.BlockSpec((None, None, 2, bt, sw), lambda s, i: (s, i, 0, 0, 0))],
        out_specs=[tokens, pl.BlockSpec((None, None, bt, sw), lambda s, i: (s, i, 0, 0))],
        out_shape=[jax.ShapeDtypeStruct((n, D_MODEL), F32), jax.ShapeDtypeStruct((N_SLABS, n_bt, bt, sw), F32)],
        scratch_shapes=[pltpu.VMEM((wide, wide), BF16), pltpu.VMEM((wide, sw), BF16), pltpu.VMEM((sw, wide), BF16),
                        pltpu.VMEM((2 * SLAB_GROUPS, rows, LANES), F32), pltpu.VMEM((SLAB_GROUPS, rows, LANES), F32)],
        compiler_params=_params(2),
        name="s5_core",
    )(u, w_toep, w_in, w_out, coef, h0)
    hfin = hfin.transpose(1, 2, 0, 3).reshape(nb, N_GROUPS, 2 * P_STATE)
    return y, hfin[..., :P_STATE], hfin[..., P_STATE:]


def _rope_tables(pos, rows):
    half = HEAD_DIM // 2
    inv = ROPE_THETA ** (-jnp.arange(half, dtype=F32) / half)
    ang = pos.astype(F32)[:, None] * inv[None, :]
    cos = jnp.cos(ang)
    sin = jnp.sin(ang)
    cos_t = jnp.concatenate([cos, cos, cos, cos], axis=-1)
    sin_t = jnp.concatenate([-sin, sin, -sin, sin], axis=-1)
    reps = rows // pos.shape[0]
    return jnp.tile(cos_t, (reps, 1)), jnp.tile(sin_t, (reps, 1))


def _rope_slab(v, cos, sin, first_half):
    partner = jnp.where(first_half, pltpu.roll(v, LANES - HEAD_DIM // 2, 1), pltpu.roll(v, HEAD_DIM // 2, 1))
    return v * cos + partner * sin


def _first_half_mask(rows):
    lane = lax.broadcasted_iota(jnp.int32, (rows, LANES), 1)
    return (lane % HEAD_DIM) < (HEAD_DIM // 2)


def _kv_rope_body(x_ref, wk_ref, wv_ref, cos_ref, sin_ref, kt_ref, ktb_ref, v_ref, vb_ref):
    xb = x_ref[...].astype(BF16)
    tm = xb.shape[0]
    half = HEAD_DIM // 2
    cos = cos_ref[...]
    sin = sin_ref[...]
    kt = _dot_nt(wk_ref[...], xb)
    for h in range(N_HEADS):
        for mp in range(2):
            lo = (2 * h + mp) * HEAD_DIM
            x1 = kt[lo:lo + half]
            x2 = kt[lo + half:lo + HEAD_DIM]
            r1 = x1 * cos - x2 * sin
            r2 = x1 * sin + x2 * cos
            kt_ref[h, mp, pl.ds(0, half), :] = r1
            kt_ref[h, mp, pl.ds(half, half), :] = r2
            ktb_ref[h, pl.ds(mp * HEAD_DIM, half), :] = r1.astype(BF16)
            ktb_ref[h, pl.ds(mp * HEAD_DIM + half, half), :] = r2.astype(BF16)
    v_all = _dot(xb, wv_ref[...])
    vb_ref[...] = v_all.astype(BF16)
    for h in range(N_HEADS):
        v_ref[pl.ds(h, tm, stride=N_HEADS), :] = v_all[:, h * V_DIM:(h + 1) * V_DIM]


def _kv_rope(x, nb, w_kv, pos):
    n = x.shape[0]
    length = n // nb
    tm = min(ROW_TILE, length)
    tiles = length // tm
    half = HEAD_DIM // 2
    inv = ROPE_THETA ** (-jnp.arange(half, dtype=F32) / half)
    ang = inv[:, None] * pos.astype(F32)[None, :]
    wk_t = w_kv[:, :D_MODEL].T
    w_v = w_kv[:, D_MODEL:]
    table = pl.BlockSpec((half, tm), lambda i: (0, i % tiles))
    return pl.pallas_call(
        _kv_rope_body,
        grid=(n // tm,),
        in_specs=[pl.BlockSpec((tm, D_MODEL), lambda i: (i, 0)), _const_spec((D_MODEL, D_MODEL)),
                  _const_spec((D_MODEL, D_MODEL)), table, table],
        out_specs=[pl.BlockSpec((None, N_HEADS, 2, HEAD_DIM, tm), lambda i: (i // tiles, 0, 0, 0, i % tiles)),
                   pl.BlockSpec((None, N_HEADS, V_DIM, tm), lambda i: (i // tiles, 0, 0, i % tiles)),
                   pl.BlockSpec((tm * N_HEADS, V_DIM), lambda i: (i, 0)),
                   pl.BlockSpec((tm, D_MODEL), lambda i: (i, 0))],
        out_shape=[jax.ShapeDtypeStruct((nb, N_HEADS, 2, HEAD_DIM, length), F32),
                   jax.ShapeDtypeStruct((nb, N_HEADS, V_DIM, length), BF16),
                   jax.ShapeDtypeStruct((n * N_HEADS, V_DIM), F32),
                   jax.ShapeDtypeStruct((n, D_MODEL), BF16)],
        compiler_params=_params(1),
        name="kv_rope",
    )(x, wk_t, w_v, jnp.cos(ang), jnp.sin(ang))


def _lam_value(lam_ref, layer_idx):
    lv = lam_ref[...]
    s01 = jnp.sum(lv[0:1, :] * lv[1:2, :], axis=-1, keepdims=True)
    s23 = jnp.sum(lv[2:3, :] * lv[3:4, :], axis=-1, keepdims=True)
    lam_init = 0.8 - 0.6 * math.exp(-0.3 * layer_idx)
    return jnp.exp(s01) - jnp.exp(s23) + lam_init, lam_init


def _diff_attend(q, segments, lam, lam_init, subln_g):
    lane = lax.broadcasted_iota(jnp.int32, q.shape, 1)
    zero = jnp.zeros_like(q)
    q1 = jnp.where(lane < HEAD_DIM, q, zero)
    q2 = jnp.where(lane >= HEAD_DIM, q, zero)
    maps = []
    for qm in (q1, q2):
        scores = []
        for kt, _, mask in segments:
            s = _dot(qm, kt)
            scores.append(s if mask is None else jnp.where(mask, s, NEG_INF))
        m = functools.reduce(jnp.maximum, [jnp.max(s, axis=-1, keepdims=True) for s in scores])
        acc = functools.reduce(jnp.add, [_dot(jnp.exp2(s - m).astype(BF16), v) for s, (_, v, _) in zip(scores, segments)])
        maps.append(acc[:, :V_DIM] / acc[:, V_DIM:V_DIM + 1])
    out = maps[0] - lam * maps[1]
    ms = jnp.mean(out * out, axis=-1, keepdims=True)
    return out * lax.rsqrt(ms + RMS_EPS) * subln_g * (1.0 - lam_init)


def _attn_prompt_body(q_ref, kt_ref, v_ref, lam_ref, g_ref, o_ref, *, length, tq, layer_idx):
    lam, lam_init = _lam_value(lam_ref, layer_idx)
    g = g_ref[...]
    r = lax.broadcasted_iota(jnp.int32, (tq, tq), 0) // CHUNK
    c = lax.broadcasted_iota(jnp.int32, (tq, tq), 1) // CHUNK
    diag_mask = c <= r
    ones = jnp.ones((length, V_DIM), BF16)
    v_ext = [jnp.concatenate([v_ref[:, pl.ds(hh * V_DIM, V_DIM)], ones], axis=1) for hh in range(ATTN_HEADS)]
    for i in range(length // tq):
        rows = pl.ds(i * tq, tq)
        for hh in range(ATTN_HEADS):
            cols = pl.ds(hh * V_DIM, V_DIM)
            segments = []
            if i > 0:
                segments.append((kt_ref[hh, :, pl.ds(0, i * tq)], v_ext[hh][:i * tq], None))
            segments.append((kt_ref[hh, :, rows], v_ext[hh][i * tq:(i + 1) * tq], diag_mask))
            o_ref[rows, cols] = _diff_attend(q_ref[rows, cols], segments, lam, lam_init, g).astype(BF16)


def _attn_prompt(q, ktb, vb, attn_lam, subln_g, layer_idx):
    nb, length, _ = q.shape
    tq = min(Q_TILE, length)
    head = pl.BlockSpec((None, length, ATTN_HEADS * V_DIM), lambda b, h: (b, 0, h))
    head_t = pl.BlockSpec((None, ATTN_HEADS, V_DIM, length), lambda b, h: (b, h, 0, 0))
    return pl.pallas_call(
        functools.partial(_attn_prompt_body, length=length, tq=tq, layer_idx=layer_idx),
        grid=(nb, N_HEADS // ATTN_HEADS),
        in_specs=[head, head_t, head, _const_spec((4, HEAD_DIM)), _const_spec((1, V_DIM))],
        out_specs=head,
        out_shape=jax.ShapeDtypeStruct((nb, length, D_MODEL), BF16),
        compiler_params=_params(2),
        name="attn_prompt",
    )(q, ktb, vb, attn_lam, subln_g)


def _attn_sample_body(q_ref, kp_ref, vp_ref, kt_ref, v_ref, lam_ref, g_ref, o_ref, m_ref, l_ref, acc_ref, *, layer_idx, tp):
    pt = pl.program_id(1)
    n_q = q_ref.shape[0]
    zero = jnp.zeros((n_q, HEAD_DIM), BF16)

    def head_scores(h, k1, k2):
        q1 = q_ref[:, h * V_DIM:h * V_DIM + HEAD_DIM]
        q2 = q_ref[:, h * V_DIM + HEAD_DIM:(h + 1) * V_DIM]
        return _dot(jnp.concatenate([q1, zero], axis=0), k1) + _dot(jnp.concatenate([zero, q2], axis=0), k2)

    def accumulate(keys_of, values_of, first):
        s = jnp.concatenate([head_scores(h, *keys_of(h)) for h in range(N_HEADS)], axis=0)
        m_tile = jnp.max(s, axis=-1, keepdims=True)
        if first:
            m_new = m_tile
        else:
            m_old = m_ref[...]
            m_new = jnp.maximum(m_old, m_tile)
            alpha = jnp.exp2(m_old - m_new)
        e = jnp.exp2(s - m_new)
        l_tile = jnp.sum(e, axis=-1, keepdims=True)
        eb = e.astype(BF16)
        pv = jnp.concatenate([_dot(eb[2 * n_q * h:2 * n_q * (h + 1)], values_of(h)) for h in range(N_HEADS)], axis=0)
        m_ref[...] = m_new
        l_ref[...] = l_tile if first else alpha * l_ref[...] + l_tile
        acc_ref[...] = pv if first else alpha * acc_ref[...] + pv

    @pl.when(pt == 0)
    def _():
        accumulate(lambda h: (kt_ref[h, pl.ds(0, HEAD_DIM), :], kt_ref[h, pl.ds(HEAD_DIM, HEAD_DIM), :]),
                   lambda h: v_ref[:, h * V_DIM:(h + 1) * V_DIM], True)

    accumulate(lambda h: (kp_ref[h, 0].astype(BF16), kp_ref[h, 1].astype(BF16)),
               lambda h: vp_ref[pl.ds(h, tp, stride=N_HEADS), :].astype(BF16), False)

    @pl.when(pt == pl.num_programs(1) - 1)
    def _():
        lam, lam_init = _lam_value(lam_ref, layer_idx)
        g = g_ref[...]
        for h in range(N_HEADS):
            r1 = pl.ds(2 * n_q * h, n_q)
            r2 = pl.ds(2 * n_q * h + n_q, n_q)
            out = acc_ref[r1, :] / l_ref[r1, :] - acc_ref[r2, :] * (lam / l_ref[r2, :])
            ms = jnp.mean(out * out, axis=-1, keepdims=True)
            o_ref[:, h * V_DIM:(h + 1) * V_DIM] = (out * lax.rsqrt(ms + RMS_EPS) * g * (1.0 - lam_init)).astype(BF16)


def _attn_sample(q, cache_k, cache_v, ktb, vb, attn_lam, subln_g, layer_idx):
    nb, length, _ = q.shape
    past = cache_k.shape[1]
    assert past % CHUNK == 0 and length <= CHUNK
    tp = PAST_TILE if past % PAST_TILE == 0 else past
    new = pl.BlockSpec((None, length, D_MODEL), lambda b, p: (b, 0, 0))
    hm = 2 * N_HEADS
    return pl.pallas_call(
        functools.partial(_attn_sample_body, layer_idx=layer_idx, tp=tp),
        grid=(nb, past // tp),
        in_specs=[new,
                  pl.BlockSpec((None, N_HEADS, 2, HEAD_DIM, tp), lambda b, p: (b, 0, 0, 0, p)),
                  pl.BlockSpec((None, tp * N_HEADS, V_DIM), lambda b, p: (b, p, 0)),
                  pl.BlockSpec((None, N_HEADS, V_DIM, length), lambda b, p: (b, 0, 0, 0)),
                  new, _const_spec((4, HEAD_DIM)), _const_spec((1, V_DIM))],
        out_specs=new,
        out_shape=jax.ShapeDtypeStruct((nb, length, D_MODEL), BF16),
        scratch_shapes=[pltpu.VMEM((hm * length, 1), F32), pltpu.VMEM((hm * length, 1), F32),
                        pltpu.VMEM((hm * length, V_DIM), F32)],
        compiler_params=_params(2),
        name="attn_sample",
    )(q, cache_k.transpose(0, 2, 3, 4, 1), cache_v.reshape(nb, past * N_HEADS, V_DIM), ktb, vb, attn_lam, subln_g)


def _run_trunk(x, pos, h0_re, h0_im, cache_k, cache_v, w):
    nb, length, _ = x.shape
    n = nb * length
    assert length % S5_T == 0
    ln_g, ln_b = w['ln_g'], w['ln_b']

    def vec(v):
        return v.reshape(1, D_MODEL).astype(F32)

    def ffn(t, l, j, ln_idx, **fused):
        return _ffn_ln(t, w['ffn_w_in'][l, j], w['ffn_w_out'][l, j], vec(ln_g[l, ln_idx]), vec(ln_b[l, ln_idx]), **fused)

    t = x.reshape(n, D_MODEL)
    new_re, new_im = [], []
    kt = v = None
    for l in range(DEPTH):
        if l == N_A:
            kt, ktb, v, vb = _kv_rope(t, nb, w['attn_w_kv'], pos)
        if l < N_A:
            t = ffn(t, l, 0, 0)
            ys, hr, hi = _s5_core(t, nb, h0_re[l], h0_im[l], w['ssm'][l])
            new_re.append(hr)
            new_im.append(hi)
            mixer = ("glu", ys, vec(w['ssm_d'][l]), w['ssm_w_glu'][l], vec(ln_g[l, 1]), vec(ln_b[l, 1]))
        else:
            j = l - N_A
            t, q = ffn(t, l, 0, 0, post=("q", w['attn_w_q'][j], pos, length))
            q = q.reshape(nb, length, D_MODEL)
            vb3 = vb.reshape(nb, length, D_MODEL)
            lam_p = w['attn_lam'][j].astype(F32)
            sub_g = w['attn_subln_g'][j].reshape(1, V_DIM).astype(F32)
            if cache_k is None:
                o = _attn_prompt(q, ktb, vb3, lam_p, sub_g, l)
            else:
                o = _attn_sample(q, cache_k, cache_v, ktb, vb3, lam_p, sub_g, l)
            mixer = ("proj", o.reshape(n, D_MODEL), w['attn_w_o'][j], vec(ln_g[l, 1]), vec(ln_b[l, 1]))
        t = ffn(t, l, 1, 2, pre=mixer)
    y = t.reshape(nb, length, D_MODEL)
    p_k = kt.transpose(0, 4, 1, 2, 3)
    p_v = v.reshape(nb, length, N_HEADS, V_DIM)
    return y, jnp.stack(new_re), jnp.stack(new_im), p_k, p_v


def kernel(x_prompt, x_sample, cache_k, cache_v, state_ssm_re, state_ssm_im,
           ln_g, ln_b, ffn_w_in, ffn_w_out,
           ssm_a_re, ssm_a_im, ssm_log_dt, ssm_b_re, ssm_b_im, ssm_c_re, ssm_c_im, ssm_d, ssm_w_glu,
           attn_w_q, attn_w_kv, attn_lam, attn_subln_g, attn_w_o):
    w = {'ln_g': ln_g, 'ln_b': ln_b,
         'ffn_w_in': ffn_w_in.astype(BF16), 'ffn_w_out': ffn_w_out.astype(BF16),
         'ssm': [_s5_weights(ssm_a_re[l], ssm_a_im[l], ssm_log_dt[l], ssm_b_re[l], ssm_b_im[l],
                             ssm_c_re[l], ssm_c_im[l]) for l in range(N_A)],
         'ssm_d': ssm_d, 'ssm_w_glu': ssm_w_glu.astype(BF16),
         'attn_w_q': attn_w_q.astype(BF16), 'attn_w_kv': attn_w_kv.astype(BF16), 'attn_lam': attn_lam,
         'attn_subln_g': attn_subln_g, 'attn_w_o': attn_w_o.astype(BF16)}
    b_p, l_p, _ = x_prompt.shape
    pos_p = jnp.arange(l_p, dtype=jnp.int32)
    h0 = jnp.zeros((N_A, b_p, N_GROUPS, P_STATE), F32)
    y_p, p_re, p_im, p_k, p_v = _run_trunk(x_prompt, pos_p, h0, h0, None, None, w)
    past = cache_k.shape[1]
    pos_s = past + jnp.arange(x_sample.shape[1], dtype=jnp.int32)
    y_s, s_re, s_im, s_k, s_v = _run_trunk(x_sample, pos_s, state_ssm_re, state_ssm_im, cache_k, cache_v, w)
    return (y_p, y_s, p_re, p_im, p_k, p_v, s_re, s_im, s_k, s_v)
```

```python
import functools
import math

import jax
import jax.numpy as jnp
from jax import lax
from jax.experimental import pallas as pl
from jax.experimental.pallas import tpu as pltpu

F32 = jnp.float32
BF16 = jnp.bfloat16

D_MODEL = 1024
DEPTH = 2
CHUNK = 64
N_A = DEPTH // 2
GROUP = 16
N_GROUPS = D_MODEL // GROUP
P_STATE = 64
HEAD_DIM = 64
V_DIM = 2 * HEAD_DIM
N_HEADS = D_MODEL // V_DIM
D_FF = ((8 * D_MODEL // 3 + 127) // 128) * 128
ROPE_THETA = 10000.0
ALPHA = (2 * DEPTH) ** 0.25
LN_EPS = 1e-5
RMS_EPS = 1e-5
NEG_INF = -1e30
LOG2_E = math.log2(math.e)

LANES = 128
ROW_TILE = 512
FF_TILE = 256
S5_T = 8
SLAB_GROUPS = LANES // GROUP
N_SLABS = N_GROUPS // SLAB_GROUPS
S5_ROWS = 1024
Q_TILE = 512
ATTN_HEADS = 2
PAST_TILE = 1024
VMEM_LIMIT = 56 * 1024 * 1024


def _params(n_axes):
    return pltpu.CompilerParams(dimension_semantics=("arbitrary",) * n_axes, vmem_limit_bytes=VMEM_LIMIT)


def _row_tile(n):
    return ROW_TILE if n % ROW_TILE == 0 else n


def _const_spec(shape):
    return pl.BlockSpec(shape, lambda *_: (0,) * len(shape), pipeline_mode=pl.Buffered(1))


def _layer_norm(v, g, b):
    mu = jnp.mean(v, axis=-1, keepdims=True)
    c = v - mu
    var = jnp.mean(c * c, axis=-1, keepdims=True)
    return c * lax.rsqrt(var + LN_EPS) * g + b


def _sigmoid(v):
    return 0.5 * jnp.tanh(0.5 * v) + 0.5


def _gelu_tanh(v):
    return 0.5 * v * (1.0 + jnp.tanh(math.sqrt(2.0 / math.pi) * (v + 0.044715 * (v * v * v))))


def _dot(a, b):
    return jnp.dot(a, b, preferred_element_type=F32)


def _dot_nt(a, b):
    return lax.dot_general(a, b, (((1,), (1,)), ((), ())), preferred_element_type=F32)


def _swiglu_ln(x, win_ref, wout_ref, g, b):
    xb = x.astype(BF16)
    acc = None
    for c in range(D_FF // FF_TILE):
        a = _dot(xb, win_ref[:, pl.ds(c * FF_TILE, FF_TILE)])
        gate = _dot(xb, win_ref[:, pl.ds(D_FF + c * FF_TILE, FF_TILE)])
        h = (a * _sigmoid(a) * gate).astype(BF16)
        part = _dot(h, wout_ref[pl.ds(c * FF_TILE, FF_TILE), :])
        acc = part if acc is None else acc + part
    return _layer_norm(ALPHA * x + 0.5 * acc, g, b)


def _glu_ln(x, ys, d_skip, wg_ref, g, b):
    z = _gelu_tanh(ys + d_skip * x).astype(BF16)
    out = _dot(z, wg_ref[:, pl.ds(0, D_MODEL)])
    gate = _dot(z, wg_ref[:, pl.ds(D_MODEL, D_MODEL)])
    return _layer_norm(ALPHA * x + out * _sigmoid(gate), g, b)


def _q_rope(x, wq_ref, cos, sin, q_ref):
    xb = x.astype(BF16)
    first_half = _first_half_mask(xb.shape[0])
    scale = HEAD_DIM ** -0.5 * LOG2_E
    q_all = _dot(xb, wq_ref[...])
    for j in range(D_MODEL // LANES):
        q = _rope_slab(q_all[:, j * LANES:(j + 1) * LANES], cos, sin, first_half)
        q_ref[:, pl.ds(j * LANES, LANES)] = (q * scale).astype(BF16)


def _ffn_body(*refs, pre, post):
    refs = list(refs)
    x = refs.pop(0)[...]
    if pre == "glu":
        ys_ref, d_ref, wg_ref, g0_ref, b0_ref = refs[:5]
        del refs[:5]
        x = _glu_ln(x, ys_ref[...], d_ref[...], wg_ref, g0_ref[...], b0_ref[...])
    elif pre == "proj":
        o_ref, wo_ref, g0_ref, b0_ref = refs[:4]
        del refs[:4]
        x = _layer_norm(ALPHA * x + _dot(o_ref[...], wo_ref[...]), g0_ref[...], b0_ref[...])
    win_ref, wout_ref, g_ref, b_ref = refs[:4]
    del refs[:4]
    y = _swiglu_ln(x, win_ref, wout_ref, g_ref[...], b_ref[...])
    if post == "q":
        wq_ref, cos_ref, sin_ref, y_ref, q_ref = refs
        _q_rope(y, wq_ref, cos_ref[...], sin_ref[...], q_ref)
    else:
        (y_ref,) = refs
    y_ref[...] = y


def _ffn_ln(x, w_in, w_out, g, b, pre=None, post=None):
    n = x.shape[0]
    tm = _row_tile(n)
    row = pl.BlockSpec((tm, D_MODEL), lambda i: (i, 0))
    vec = _const_spec((1, D_MODEL))
    args, specs = [x], [row]
    if pre is not None and pre[0] == "glu":
        args += list(pre[1:])
        specs += [row, vec, _const_spec((D_MODEL, 2 * D_MODEL)), vec, vec]
    elif pre is not None:
        args += list(pre[1:])
        specs += [row, _const_spec((D_MODEL, D_MODEL)), vec, vec]
    args += [w_in, w_out, g, b]
    specs += [_const_spec((D_MODEL, 2 * D_FF)), _const_spec((D_FF, D_MODEL)), vec, vec]
    out_specs, out_shape = [row], [jax.ShapeDtypeStruct((n, D_MODEL), F32)]
    if post is not None:
        _, w_q, pos, length = post
        table_rows = max(length, tm)
        assert table_rows % length == 0 and table_rows % tm == 0
        cos_t, sin_t = _rope_tables(pos, table_rows)
        n_table_tiles = table_rows // tm
        table = pl.BlockSpec((tm, LANES), lambda i: (i % n_table_tiles, 0))
        args += [w_q, cos_t, sin_t]
        specs += [_const_spec((D_MODEL, D_MODEL)), table, table]
        out_specs.append(row)
        out_shape.append(jax.ShapeDtypeStruct((n, D_MODEL), BF16))
    outs = pl.pallas_call(
        functools.partial(_ffn_body, pre=None if pre is None else pre[0], post=None if post is None else post[0]),
        grid=(n // tm,),
        in_specs=specs,
        out_specs=out_specs,
        out_shape=out_shape,
        compiler_params=_params(1),
        name="ffn_ln",
    )(*args)
    return outs if post is not None else outs[0]


def _s5_weights(a_re, a_im, log_dt, b_re, b_im, c_re, c_im):
    hi = lax.Precision.HIGHEST
    lam_re = jnp.minimum(a_re.astype(F32), -1e-4)
    lam_im = a_im.astype(F32)
    dt = jnp.exp(log_dt.astype(F32))[:, None]
    mag = jnp.exp(lam_re * dt)
    lb_re = mag * jnp.cos(lam_im * dt)
    lb_im = mag * jnp.sin(lam_im * dt)
    den = lam_re * lam_re + lam_im * lam_im
    n_re = lb_re - 1.0
    k_re = (n_re * lam_re + lb_im * lam_im) / den
    k_im = (lb_im * lam_re - n_re * lam_im) / den
    b_re = b_re.astype(F32)
    b_im = b_im.astype(F32)
    bb_re = k_re[..., None] * b_re - k_im[..., None] * b_im
    bb_im = k_re[..., None] * b_im + k_im[..., None] * b_re
    c_re = c_re.astype(F32)
    c_im = c_im.astype(F32)

    def power_step(carry, _):
        pr, pi = carry
        return (pr * lb_re - pi * lb_im, pr * lb_im + pi * lb_re), (pr, pi)

    _, (pw_re, pw_im) = lax.scan(power_step, (jnp.ones_like(lb_re), jnp.zeros_like(lb_re)), None,
                                 length=S5_T + 1)

    cl_re = c_re[None] * pw_re[:, :, None, :] - c_im[None] * pw_im[:, :, None, :]
    cl_im = c_re[None] * pw_im[:, :, None, :] + c_im[None] * pw_re[:, :, None, :]

    taps = (jnp.einsum('ngdp,gpc->ngcd', cl_re[:S5_T], bb_re, precision=hi)
            - jnp.einsum('ngdp,gpc->ngcd', cl_im[:S5_T], bb_im, precision=hi))
    state_w = SLAB_GROUPS * 2 * P_STATE

    lane_of = jnp.arange(LANES)[None, :] % GROUP
    repeat_lanes = (lane_of == jnp.arange(GROUP)[:, None]).astype(F32)
    w_toep = jnp.einsum('ngcd,dl->ngcl', taps, repeat_lanes, precision=hi)
    w_toep = w_toep.reshape(S5_T, N_SLABS, LANES, LANES).transpose(1, 0, 2, 3).astype(BF16)

    rev_re = pw_re[S5_T - 1::-1][:S5_T]
    rev_im = pw_im[S5_T - 1::-1][:S5_T]
    in_re = rev_re[..., None] * bb_re[None] - rev_im[..., None] * bb_im[None]
    in_im = rev_re[..., None] * bb_im[None] + rev_im[..., None] * bb_re[None]
    w_in = jnp.concatenate([in_re, in_im], axis=2)
    w_in = jnp.einsum('ck,sgpk->csgp', jnp.eye(GROUP, dtype=F32), w_in, precision=hi).transpose(1, 2, 0, 3)
    w_in = w_in.reshape(S5_T, N_SLABS, LANES, 2 * P_STATE).transpose(1, 0, 2, 3).astype(BF16)

    w_out = jnp.concatenate([cl_re[1:], -cl_im[1:]], axis=3)
    w_out = w_out.reshape(S5_T, N_SLABS, SLAB_GROUPS, GROUP, 2 * P_STATE).transpose(1, 0, 4, 2, 3)
    w_out = w_out.reshape(N_SLABS, S5_T, 2 * P_STATE, LANES).astype(BF16)

    lt_re, lt_im = pw_re[S5_T], pw_im[S5_T]
    coef = jnp.stack([jnp.concatenate([lt_re, lt_re], -1),
                      jnp.concatenate([-lt_im, lt_im], -1),
                      jnp.concatenate([lt_im, -lt_im], -1)], axis=0)
    coef = coef.reshape(3, N_SLABS, state_w).transpose(1, 0, 2)
    coef = jnp.concatenate([coef, jnp.zeros((N_SLABS, 5, state_w), F32)], axis=1)
    return w_toep, w_in, w_out, coef


def _s5_expand_weights(toep_ref, cin_ref, cout_ref, wt_ref, win_ref, wout_ref):
    group_of_row = lax.broadcasted_iota(jnp.int32, (LANES, LANES), 0) // GROUP
    group_of_col = lax.broadcasted_iota(jnp.int32, (LANES, LANES), 1) // GROUP
    zero = jnp.zeros((LANES, LANES), BF16)
    taps = [jnp.where(group_of_row == group_of_col, toep_ref[n], zero) for n in range(S5_T)]
    for s in range(S5_T):
        rows = pl.ds(s * LANES, LANES)
        for t in range(S5_T):
            wt_ref[rows, pl.ds(t * LANES, LANES)] = taps[t - s] if t >= s else zero
        tile_in = cin_ref[s]
        tile_out = cout_ref[s]
        for j in range(SLAB_GROUPS):
            win_ref[rows, pl.ds(j * LANES, LANES)] = jnp.where(group_of_row == j, tile_in, zero)
            wout_ref[pl.ds(j * LANES, LANES), rows] = jnp.where(group_of_col == j, tile_out, zero)


def _s5_body(x_ref, toep_ref, cin_ref, cout_ref, coef_ref, h0_ref, y_ref, hfin_ref,
             wt_ref, win_ref, wout_ref, s_ref, hst_ref, *, rows, n_chunks, bt):
    @pl.when(pl.program_id(1) == 0)
    def _():
        _s5_expand_weights(toep_ref, cin_ref, cout_ref, wt_ref, win_ref, wout_ref)

    xcat = jnp.concatenate([x_ref[pl.ds(s, rows, stride=S5_T), :].astype(BF16) for s in range(S5_T)], axis=1)
    s_all = _dot(xcat, win_ref[...])
    for j in range(SLAB_GROUPS):
        sj = s_all[:, j * LANES:(j + 1) * LANES]
        sj_t = pltpu.roll(sj, P_STATE, 1)
        for b in range(bt):
            chunk_major = pl.ds(b, n_chunks, stride=bt)
            s_ref[j, chunk_major, :] = sj[b * n_chunks:(b + 1) * n_chunks]
            s_ref[SLAB_GROUPS + j, chunk_major, :] = sj_t[b * n_chunks:(b + 1) * n_chunks]
    coef = coef_ref[...]
    h0 = h0_ref[...]

    def lanes(v, j):
        return v[:, j * LANES:(j + 1) * LANES]

    def step(k, carry):
        hs, hts = carry
        idx = pl.ds(k * bt, bt)
        new_h, new_ht = [], []
        for j in range(SLAB_GROUPS):
            hst_ref[j, idx, :] = hs[j]
            c_self, c_swap, c_swap_t = lanes(coef[0:1], j), lanes(coef[1:2], j), lanes(coef[2:3], j)
            new_h.append(c_self * hs[j] + c_swap * hts[j] + s_ref[j, idx, :])
            new_ht.append(c_self * hts[j] + c_swap_t * hs[j] + s_ref[SLAB_GROUPS + j, idx, :])
        return tuple(new_h), tuple(new_ht)

    init = (tuple(lanes(h0[0], j) for j in range(SLAB_GROUPS)), tuple(lanes(h0[1], j) for j in range(SLAB_GROUPS)))
    carry = init
    for k in range(n_chunks):
        carry = step(k, carry)
    hs, _ = carry
    hfin_ref[...] = jnp.concatenate(hs, axis=1)
    hst = jnp.concatenate(
        [jnp.concatenate([hst_ref[j, pl.ds(b, n_chunks, stride=bt), :] for b in range(bt)], axis=0)
         for j in range(SLAB_GROUPS)], axis=1).astype(BF16)
    for t in range(0, S5_T, 2):
        cols = pl.ds(t * LANES, 2 * LANES)
        k_used = (t + 2) * LANES
        y = _dot(xcat[:, :k_used], wt_ref[pl.ds(0, k_used), cols]) + _dot(hst, wout_ref[:, cols])
        y_ref[pl.ds(t, rows, stride=S5_T), :] = y[:, :LANES]
        y_ref[pl.ds(t + 1, rows, stride=S5_T), :] = y[:, LANES:]


def _s5_batch_tile(nb, n_chunks):
    bt = nb
    while bt * n_chunks > S5_ROWS and bt % 2 == 0:
        bt //= 2
    return bt


def _s5_core(u, nb, h0_re, h0_im, weights):
    w_toep, w_in, w_out, coef = weights
    n = u.shape[0]
    length = n // nb
    n_chunks = length // S5_T
    bt = _s5_batch_tile(nb, n_chunks)
    n_bt = nb // bt
    rows = bt * n_chunks
    wide = S5_T * LANES
    sw = SLAB_GROUPS * 2 * P_STATE

    def slab_state(re, im):
        v = jnp.concatenate([re, im], axis=-1).astype(F32)
        return v.reshape(n_bt, bt, N_SLABS, sw).transpose(2, 0, 1, 3)

    h0 = jnp.stack([slab_state(h0_re, h0_im), slab_state(h0_im, h0_re)], axis=2)

    def per_slab(shape):
        return pl.BlockSpec((None,) + shape, lambda s, i: (s,) + (0,) * len(shape), pipeline_mode=pl.Buffered(1))

    tokens = pl.BlockSpec((bt * length, LANES), lambda s, i: (i, s))
    y, hfin = pl.pallas_call(
        functools.partial(_s5_body, rows=rows, n_chunks=n_chunks, bt=bt),
        grid=(N_SLABS, n_bt),
        in_specs=[tokens, per_slab((S5_T, LANES, LANES)), per_slab((S5_T, LANES, 2 * P_STATE)),
                  per_slab((S5_T, 2 * P_STATE, LANES)), per_slab((8, sw)),
                  pl.BlockSpec((None, None, 2, bt, sw), lambda s, i: (s, i, 0, 0, 0))],
        out_specs=[tokens, pl.BlockSpec((None, None, bt, sw), lambda s, i: (s, i, 0, 0))],
        out_shape=[jax.ShapeDtypeStruct((n, D_MODEL), F32), jax.ShapeDtypeStruct((N_SLABS, n_bt, bt, sw), F32)],
        scratch_shapes=[pltpu.VMEM((wide, wide), BF16), pltpu.VMEM((wide, sw), BF16), pltpu.VMEM((sw, wide), BF16),
                        pltpu.VMEM((2 * SLAB_GROUPS, rows, LANES), F32), pltpu.VMEM((SLAB_GROUPS, rows, LANES), F32)],
        compiler_params=_params(2),
        name="s5_core",
    )(u, w_toep, w_in, w_out, coef, h0)
    hfin = hfin.transpose(1, 2, 0, 3).reshape(nb, N_GROUPS, 2 * P_STATE)
    return y, hfin[..., :P_STATE], hfin[..., P_STATE:]


def _rope_tables(pos, rows):
    half = HEAD_DIM // 2
    inv = ROPE_THETA ** (-jnp.arange(half, dtype=F32) / half)
    ang = pos.astype(F32)[:, None] * inv[None, :]
    cos = jnp.cos(ang)
    sin = jnp.sin(ang)
    cos_t = jnp.concatenate([cos, cos, cos, cos], axis=-1)
    sin_t = jnp.concatenate([-sin, sin, -sin, sin], axis=-1)
    reps = rows // pos.shape[0]
    return jnp.tile(cos_t, (reps, 1)), jnp.tile(sin_t, (reps, 1))


def _rope_slab(v, cos, sin, first_half):
    partner = jnp.where(first_half, pltpu.roll(v, LANES - HEAD_DIM // 2, 1), pltpu.roll(v, HEAD_DIM // 2, 1))
    return v * cos + partner * sin


def _first_half_mask(rows):
    lane = lax.broadcasted_iota(jnp.int32, (rows, LANES), 1)
    return (lane % HEAD_DIM) < (HEAD_DIM // 2)


def _kv_rope_body(x_ref, wk_ref, wv_ref, cos_ref, sin_ref, kt_ref, ktb_ref, v_ref, vb_ref):
    xb = x_ref[...].astype(BF16)
    tm = xb.shape[0]
    half = HEAD_DIM // 2
    cos = cos_ref[...]
    sin = sin_ref[...]
    kt = _dot_nt(wk_ref[...], xb)
    for h in range(N_HEADS):
        for mp in range(2):
            lo = (2 * h + mp) * HEAD_DIM
            x1 = kt[lo:lo + half]
            x2 = kt[lo + half:lo + HEAD_DIM]
            r1 = x1 * cos - x2 * sin
            r2 = x1 * sin + x2 * cos
            kt_ref[h, mp, pl.ds(0, half), :] = r1
            kt_ref[h, mp, pl.ds(half, half), :] = r2
            ktb_ref[h, pl.ds(mp * HEAD_DIM, half), :] = r1.astype(BF16)
            ktb_ref[h, pl.ds(mp * HEAD_DIM + half, half), :] = r2.astype(BF16)
    v_all = _dot(xb, wv_ref[...])
    vb_ref[...] = v_all.astype(BF16)
    for h in range(N_HEADS):
        v_ref[pl.ds(h, tm, stride=N_HEADS), :] = v_all[:, h * V_DIM:(h + 1) * V_DIM]


def _kv_rope(x, nb, w_kv, pos):
    n = x.shape[0]
    length = n // nb
    tm = min(ROW_TILE, length)
    tiles = length // tm
    half = HEAD_DIM // 2
    inv = ROPE_THETA ** (-jnp.arange(half, dtype=F32) / half)
    ang = inv[:, None] * pos.astype(F32)[None, :]
    wk_t = w_kv[:, :D_MODEL].T
    w_v = w_kv[:, D_MODEL:]
    table = pl.BlockSpec((half, tm), lambda i: (0, i % tiles))
    return pl.pallas_call(
        _kv_rope_body,
        grid=(n // tm,),
        in_specs=[pl.BlockSpec((tm, D_MODEL), lambda i: (i, 0)), _const_spec((D_MODEL, D_MODEL)),
                  _const_spec((D_MODEL, D_MODEL)), table, table],
        out_specs=[pl.BlockSpec((None, N_HEADS, 2, HEAD_DIM, tm), lambda i: (i // tiles, 0, 0, 0, i % tiles)),
                   pl.BlockSpec((None, N_HEADS, V_DIM, tm), lambda i: (i // tiles, 0, 0, i % tiles)),
                   pl.BlockSpec((tm * N_HEADS, V_DIM), lambda i: (i, 0)),
                   pl.BlockSpec((tm, D_MODEL), lambda i: (i, 0))],
        out_shape=[jax.ShapeDtypeStruct((nb, N_HEADS, 2, HEAD_DIM, length), F32),
                   jax.ShapeDtypeStruct((nb, N_HEADS, V_DIM, length), BF16),
                   jax.ShapeDtypeStruct((n * N_HEADS, V_DIM), F32),
                   jax.ShapeDtypeStruct((n, D_MODEL), BF16)],
        compiler_params=_params(1),
        name="kv_rope",
    )(x, wk_t, w_v, jnp.cos(ang), jnp.sin(ang))


def _lam_value(lam_ref, layer_idx):
    lv = lam_ref[...]
    s01 = jnp.sum(lv[0:1, :] * lv[1:2, :], axis=-1, keepdims=True)
    s23 = jnp.sum(lv[2:3, :] * lv[3:4, :], axis=-1, keepdims=True)
    lam_init = 0.8 - 0.6 * math.exp(-0.3 * layer_idx)
    return jnp.exp(s01) - jnp.exp(s23) + lam_init, lam_init


def _diff_attend(q, segments, lam, lam_init, subln_g):
    lane = lax.broadcasted_iota(jnp.int32, q.shape, 1)
    zero = jnp.zeros_like(q)
    q1 = jnp.where(lane < HEAD_DIM, q, zero)
    q2 = jnp.where(lane >= HEAD_DIM, q, zero)
    maps = []
    for qm in (q1, q2):
        scores = []
        for kt, _, mask in segments:
            s = _dot(qm, kt)
            scores.append(s if mask is None else jnp.where(mask, s, NEG_INF))
        m = functools.reduce(jnp.maximum, [jnp.max(s, axis=-1, keepdims=True) for s in scores])
        acc = functools.reduce(jnp.add, [_dot(jnp.exp2(s - m).astype(BF16), v) for s, (_, v, _) in zip(scores, segments)])
        maps.append(acc[:, :V_DIM] / acc[:, V_DIM:V_DIM + 1])
    out = maps[0] - lam * maps[1]
    ms = jnp.mean(out * out, axis=-1, keepdims=True)
    return out * lax.rsqrt(ms + RMS_EPS) * subln_g * (1.0 - lam_init)


def _attn_prompt_body(q_ref, kt_ref, v_ref, lam_ref, g_ref, o_ref, *, length, tq, layer_idx):
    lam, lam_init = _lam_value(lam_ref, layer_idx)
    g = g_ref[...]
    r = lax.broadcasted_iota(jnp.int32, (tq, tq), 0) // CHUNK
    c = lax.broadcasted_iota(jnp.int32, (tq, tq), 1) // CHUNK
    diag_mask = c <= r
    ones = jnp.ones((length, V_DIM), BF16)
    v_ext = [jnp.concatenate([v_ref[:, pl.ds(hh * V_DIM, V_DIM)], ones], axis=1) for hh in range(ATTN_HEADS)]
    for i in range(length // tq):
        rows = pl.ds(i * tq, tq)
        for hh in range(ATTN_HEADS):
            cols = pl.ds(hh * V_DIM, V_DIM)
            segments = []
            if i > 0:
                segments.append((kt_ref[hh, :, pl.ds(0, i * tq)], v_ext[hh][:i * tq], None))
            segments.append((kt_ref[hh, :, rows], v_ext[hh][i * tq:(i + 1) * tq], diag_mask))
            o_ref[rows, cols] = _diff_attend(q_ref[rows, cols], segments, lam, lam_init, g).astype(BF16)


def _attn_prompt(q, ktb, vb, attn_lam, subln_g, layer_idx):
    nb, length, _ = q.shape
    tq = min(Q_TILE, length)
    head = pl.BlockSpec((None, length, ATTN_HEADS * V_DIM), lambda b, h: (b, 0, h))
    head_t = pl.BlockSpec((None, ATTN_HEADS, V_DIM, length), lambda b, h: (b, h, 0, 0))
    return pl.pallas_call(
        functools.partial(_attn_prompt_body, length=length, tq=tq, layer_idx=layer_idx),
        grid=(nb, N_HEADS // ATTN_HEADS),
        in_specs=[head, head_t, head, _const_spec((4, HEAD_DIM)), _const_spec((1, V_DIM))],
        out_specs=head,
        out_shape=jax.ShapeDtypeStruct((nb, length, D_MODEL), BF16),
        compiler_params=_params(2),
        name="attn_prompt",
    )(q, ktb, vb, attn_lam, subln_g)


def _attn_sample_body(q_ref, kp_ref, vp_ref, kt_ref, v_ref, lam_ref, g_ref, o_ref, m_ref, l_ref, acc_ref, *, layer_idx, tp):
    pt = pl.program_id(1)
    n_q = q_ref.shape[0]
    zero = jnp.zeros((n_q, HEAD_DIM), BF16)

    def head_scores(h, k1, k2):
        q1 = q_ref[:, h * V_DIM:h * V_DIM + HEAD_DIM]
        q2 = q_ref[:, h * V_DIM + HEAD_DIM:(h + 1) * V_DIM]
        return _dot(jnp.concatenate([q1, zero], axis=0), k1) + _dot(jnp.concatenate([zero, q2], axis=0), k2)

    def accumulate(keys_of, values_of, first):
        s = jnp.concatenate([head_scores(h, *keys_of(h)) for h in range(N_HEADS)], axis=0)
        m_tile = jnp.max(s, axis=-1, keepdims=True)
        if first:
            m_new = m_tile
        else:
            m_old = m_ref[...]
            m_new = jnp.maximum(m_old, m_tile)
            alpha = jnp.exp2(m_old - m_new)
        e = jnp.exp2(s - m_new)
        l_tile = jnp.sum(e, axis=-1, keepdims=True)
        eb = e.astype(BF16)
        pv = jnp.concatenate([_dot(eb[2 * n_q * h:2 * n_q * (h + 1)], values_of(h)) for h in range(N_HEADS)], axis=0)
        m_ref[...] = m_new
        l_ref[...] = l_tile if first else alpha * l_ref[...] + l_tile
        acc_ref[...] = pv if first else alpha * acc_ref[...] + pv

    @pl.when(pt == 0)
    def _():
        accumulate(lambda h: (kt_ref[h, pl.ds(0, HEAD_DIM), :], kt_ref[h, pl.ds(HEAD_DIM, HEAD_DIM), :]),
                   lambda h: v_ref[:, h * V_DIM:(h + 1) * V_DIM], True)

    accumulate(lambda h: (kp_ref[h, 0].astype(BF16), kp_ref[h, 1].astype(BF16)),
               lambda h: vp_ref[pl.ds(h, tp, stride=N_HEADS), :].astype(BF16), False)

    @pl.when(pt == pl.num_programs(1) - 1)
    def _():
        lam, lam_init = _lam_value(lam_ref, layer_idx)
        g = g_ref[...]
        for h in range(N_HEADS):
            r1 = pl.ds(2 * n_q * h, n_q)
            r2 = pl.ds(2 * n_q * h + n_q, n_q)
            out = acc_ref[r1, :] / l_ref[r1, :] - acc_ref[r2, :] * (lam / l_ref[r2, :])
            ms = jnp.mean(out * out, axis=-1, keepdims=True)
            o_ref[:, h * V_DIM:(h + 1) * V_DIM] = (out * lax.rsqrt(ms + RMS_EPS) * g * (1.0 - lam_init)).astype(BF16)


def _attn_sample(q, cache_k, cache_v, ktb, vb, attn_lam, subln_g, layer_idx):
    nb, length, _ = q.shape
    past = cache_k.shape[1]
    assert past % CHUNK == 0 and length <= CHUNK
    tp = PAST_TILE if past % PAST_TILE == 0 else past
    new = pl.BlockSpec((None, length, D_MODEL), lambda b, p: (b, 0, 0))
    hm = 2 * N_HEADS
    return pl.pallas_call(
        functools.partial(_attn_sample_body, layer_idx=layer_idx, tp=tp),
        grid=(nb, past // tp),
        in_specs=[new,
                  pl.BlockSpec((None, N_HEADS, 2, HEAD_DIM, tp), lambda b, p: (b, 0, 0, 0, p)),
                  pl.BlockSpec((None, tp * N_HEADS, V_DIM), lambda b, p: (b, p, 0)),
                  pl.BlockSpec((None, N_HEADS, V_DIM, length), lambda b, p: (b, 0, 0, 0)),
                  new, _const_spec((4, HEAD_DIM)), _const_spec((1, V_DIM))],
        out_specs=new,
        out_shape=jax.ShapeDtypeStruct((nb, length, D_MODEL), BF16),
        scratch_shapes=[pltpu.VMEM((hm * length, 1), F32), pltpu.VMEM((hm * length, 1), F32),
                        pltpu.VMEM((hm * length, V_DIM), F32)],
        compiler_params=_params(2),
        name="attn_sample",
    )(q, cache_k.transpose(0, 2, 3, 4, 1), cache_v.reshape(nb, past * N_HEADS, V_DIM), ktb, vb, attn_lam, subln_g)


def _run_trunk(x, pos, h0_re, h0_im, cache_k, cache_v, w):
    nb, length, _ = x.shape
    n = nb * length
    assert length % S5_T == 0
    ln_g, ln_b = w['ln_g'], w['ln_b']

    def vec(v):
        return v.reshape(1, D_MODEL).astype(F32)

    def ffn(t, l, j, ln_idx, **fused):
        return _ffn_ln(t, w['ffn_w_in'][l, j], w['ffn_w_out'][l, j], vec(ln_g[l, ln_idx]), vec(ln_b[l, ln_idx]), **fused)

    t = x.reshape(n, D_MODEL)
    new_re, new_im = [], []
    kt = v = None
    for l in range(DEPTH):
        if l == N_A:
            kt, ktb, v, vb = _kv_rope(t, nb, w['attn_w_kv'], pos)
        if l < N_A:
            t = ffn(t, l, 0, 0)
            ys, hr, hi = _s5_core(t, nb, h0_re[l], h0_im[l], w['ssm'][l])
            new_re.append(hr)
            new_im.append(hi)
            mixer = ("glu", ys, vec(w['ssm_d'][l]), w['ssm_w_glu'][l], vec(ln_g[l, 1]), vec(ln_b[l, 1]))
        else:
            j = l - N_A
            t, q = ffn(t, l, 0, 0, post=("q", w['attn_w_q'][j], pos, length))
            q = q.reshape(nb, length, D_MODEL)
            vb3 = vb.reshape(nb, length, D_MODEL)
            lam_p = w['attn_lam'][j].astype(F32)
            sub_g = w['attn_subln_g'][j].reshape(1, V_DIM).astype(F32)
            if cache_k is None:
                o = _attn_prompt(q, ktb, vb3, lam_p, sub_g, l)
            else:
                o = _attn_sample(q, cache_k, cache_v, ktb, vb3, lam_p, sub_g, l)
            mixer = ("proj", o.reshape(n, D_MODEL), w['attn_w_o'][j], vec(ln_g[l, 1]), vec(ln_b[l, 1]))
        t = ffn(t, l, 1, 2, pre=mixer)
    y = t.reshape(nb, length, D_MODEL)
    p_k = kt.transpose(0, 4, 1, 2, 3)
    p_v = v.reshape(nb, length, N_HEADS, V_DIM)
    return y, jnp.stack(new_re), jnp.stack(new_im), p_k, p_v


def kernel(x_prompt, x_sample, cache_k, cache_v, state_ssm_re, state_ssm_im,
           ln_g, ln_b, ffn_w_in, ffn_w_out,
           ssm_a_re, ssm_a_im, ssm_log_dt, ssm_b_re, ssm_b_im, ssm_c_re, ssm_c_im, ssm_d, ssm_w_glu,
           attn_w_q, attn_w_kv, attn_lam, attn_subln_g, attn_w_o):
    w = {'ln_g': ln_g, 'ln_b': ln_b,
         'ffn_w_in': ffn_w_in.astype(BF16), 'ffn_w_out': ffn_w_out.astype(BF16),
         'ssm': [_s5_weights(ssm_a_re[l], ssm_a_im[l], ssm_log_dt[l], ssm_b_re[l], ssm_b_im[l],
                             ssm_c_re[l], ssm_c_im[l]) for l in range(N_A)],
         'ssm_d': ssm_d, 'ssm_w_glu': ssm_w_glu.astype(BF16),
         'attn_w_q': attn_w_q.astype(BF16), 'attn_w_kv': attn_w_kv.astype(BF16), 'attn_lam': attn_lam,
         'attn_subln_g': attn_subln_g, 'attn_w_o': attn_w_o.astype(BF16)}
    b_p, l_p, _ = x_prompt.shape
    pos_p = jnp.arange(l_p, dtype=jnp.int32)
    h0 = jnp.zeros((N_A, b_p, N_GROUPS, P_STATE), F32)
    y_p, p_re, p_im, p_k, p_v = _run_trunk(x_prompt, pos_p, h0, h0, None, None, w)
    past = cache_k.shape[1]
    pos_s = past + jnp.arange(x_sample.shape[1], dtype=jnp.int32)
    y_s, s_re, s_im, s_k, s_v = _run_trunk(x_sample, pos_s, state_ssm_re, state_ssm_im, cache_k, cache_v, w)
    return (y_p, y_s, p_re, p_im, p_k, p_v, s_re, s_im, s_k, s_v)
```
